```python
import jax, jax.numpy as jnp
from jax import lax
import numpy as np

D_MODEL = 1024
BATCH = 2
SEQ = 8192
DEPTH = 2

N_MIXERS = 2
N_HGRN = (DEPTH + N_MIXERS - 1) // N_MIXERS
N_CONV = DEPTH // N_MIXERS
HGRN_EXPAND = 128
HGRN_HEADS = D_MODEL // HGRN_EXPAND
HGRN_DK = HGRN_EXPAND
HGRN_FDIM = HGRN_HEADS * HGRN_DK
HGRN_DV = D_MODEL // HGRN_HEADS
HGRN_CHUNK = 64
CONV_WIDTH = 31
FFN_CONV_WIDTH = 3
D_FF = 2816
N_MOD = 6
EPS = 1e-6

kernel_name = "hybrid_hgrn2_conformer_convffn_adaln"


def rmsnorm(x, g):
    xf = x.astype(jnp.float32)
    y = xf * lax.rsqrt(jnp.mean(xf * xf, axis=-1, keepdims=True) + EPS)
    return (y * g).astype(x.dtype)


def layernorm(x, g, b):
    xf = x.astype(jnp.float32)
    mu = jnp.mean(xf, axis=-1, keepdims=True)
    var = jnp.mean(jnp.square(xf - mu), axis=-1, keepdims=True)
    y = (xf - mu) * lax.rsqrt(var + EPS)
    return (y * g + b).astype(x.dtype)


def causal_dwconv(x, w, b):
    width, ch = w.shape
    y = lax.conv_general_dilated(
        x, w[:, None, :].astype(x.dtype), window_strides=(1,), padding=[(width - 1, 0)],
        dimension_numbers=("NWC", "WIO", "NWC"), feature_group_count=ch)
    return y + b


def hgrn2_mixer(h, w_in, lb, gnorm_g, w_out):
    B, T, _ = h.shape
    nc = T // HGRN_CHUNK
    proj = h @ w_in
    q, fz, i, g = jnp.split(proj, [HGRN_FDIM, 2 * HGRN_FDIM, 2 * HGRN_FDIM + D_MODEL], axis=-1)
    q = jax.nn.silu(q.astype(jnp.float32))
    f = lb + (1.0 - lb) * jax.nn.sigmoid(fz.astype(jnp.float32))
    k = 1.0 - f
    logf = jnp.log(f)
    v = i.astype(jnp.float32)

    def to_chunks(a, d):
        a = a.reshape(B, nc, HGRN_CHUNK, HGRN_HEADS, d)
        return jnp.transpose(a, (1, 0, 3, 2, 4))

    qc, kc, gc = to_chunks(q, HGRN_DK), to_chunks(k, HGRN_DK), to_chunks(logf, HGRN_DK)
    vc = to_chunks(v, HGRN_DV)
    causal = jnp.tril(jnp.ones((HGRN_CHUNK, HGRN_CHUNK), dtype=bool))[:, :, None]

    def step(S, inp):
        qb, kb, vb, gb = inp
        bcum = jnp.cumsum(gb, axis=2)
        diff = bcum[:, :, :, None, :] - bcum[:, :, None, :, :]
        decay = jnp.exp(jnp.where(causal, diff, -jnp.inf))
        scores = jnp.einsum("bhtd,bhtsd,bhsd->bhts", qb, decay, kb)
        o = jnp.einsum("bhts,bhsv->bhtv", scores, vb)
        o = o + jnp.einsum("bhtd,bhdv->bhtv", qb * jnp.exp(bcum), S)
        bend = bcum[:, :, -1]
        S_new = jnp.exp(bend)[..., None] * S + jnp.einsum(
            "bhsd,bhsv->bhdv", kb * jnp.exp(bend[:, :, None, :] - bcum), vb)
        return S_new, o

    S0 = jnp.zeros((B, HGRN_HEADS, HGRN_DK, HGRN_DV), jnp.float32)
    _, oc = lax.scan(step, S0, (qc, kc, vc, gc))
    o = jnp.transpose(oc, (1, 0, 3, 2, 4)).reshape(B, T, HGRN_HEADS, HGRN_DV)
    o = rmsnorm(o, gnorm_g).reshape(B, T, D_MODEL)
    o = o * jax.nn.silu(g.astype(jnp.float32))
    return o.astype(h.dtype) @ w_out


def conformer_conv_mixer(h, w_in, b_in, dw_w, dw_b, ln_g, ln_b, w_out, b_out):
    u = h @ w_in + b_in
    a, gt = jnp.split(u, 2, axis=-1)
    u = a * jax.nn.sigmoid(gt)
    u = causal_dwconv(u, dw_w, dw_b)
    u = jax.nn.silu(layernorm(u, ln_g, ln_b))
    return u @ w_out + b_out


def conv_ffn(h, w_up, dw_w, dw_b, w_down):
    u = causal_dwconv(h @ w_up, dw_w, dw_b)
    a, b = jnp.split(u, 2, axis=-1)
    return (jax.nn.silu(a) * b) @ w_down


def setup_inputs(seed: int = 0) -> dict:
    key = jax.random.key(seed)
    ks = jax.random.split(key, 24)
    D, F = D_MODEL, D_FF
    nrm = lambda k, shape, s: jax.random.normal(k, shape, jnp.float32) * s
    return {
        "x": nrm(ks[0], (BATCH, SEQ, D), 1.0),
        "c": nrm(ks[1], (BATCH, D), 1.0),
        "ada_w": nrm(ks[2], (DEPTH, D, N_MOD * D), 0.5 * D ** -0.5),
        "ada_b": nrm(ks[3], (DEPTH, N_MOD * D), 0.02),
        "pre_mix_g": 1.0 + nrm(ks[4], (DEPTH, D), 0.02),
        "post_mix_g": 1.0 + nrm(ks[5], (DEPTH, D), 0.02),
        "pre_ffn_g": 1.0 + nrm(ks[6], (DEPTH, D), 0.02),
        "post_ffn_g": 1.0 + nrm(ks[7], (DEPTH, D), 0.02),
        "hgrn_w_in": nrm(ks[8], (N_HGRN, D, 2 * HGRN_FDIM + 2 * D), D ** -0.5),
        "hgrn_lb_logits": nrm(ks[9], (DEPTH + 1, HGRN_FDIM), 1.0),
        "hgrn_gnorm_g": 1.0 + nrm(ks[10], (N_HGRN, HGRN_DV), 0.02),
        "hgrn_w_out": nrm(ks[11], (N_HGRN, D, D), D ** -0.5),
        "conv_w_in": nrm(ks[12], (N_CONV, D, 2 * D), D ** -0.5),
        "conv_b_in": nrm(ks[13], (N_CONV, 2 * D), 0.02),
        "conv_dw_w": nrm(ks[14], (N_CONV, CONV_WIDTH, D), CONV_WIDTH ** -0.5),
        "conv_dw_b": nrm(ks[15], (N_CONV, D), 0.02),
        "conv_ln_g": 1.0 + nrm(ks[16], (N_CONV, D), 0.02),
        "conv_ln_b": nrm(ks[17], (N_CONV, D), 0.02),
        "conv_w_out": nrm(ks[18], (N_CONV, D, D), D ** -0.5),
        "conv_b_out": nrm(ks[19], (N_CONV, D), 0.02),
        "ffn_w_up": nrm(ks[20], (DEPTH, D, 2 * F), D ** -0.5),
        "ffn_dw_w": nrm(ks[21], (DEPTH, FFN_CONV_WIDTH, 2 * F), FFN_CONV_WIDTH ** -0.5),
        "ffn_dw_b": nrm(ks[22], (DEPTH, 2 * F), 0.02),
        "ffn_w_down": nrm(ks[23], (DEPTH, F, D), F ** -0.5),
    }


def reference(x, c, ada_w, ada_b, pre_mix_g, post_mix_g, pre_ffn_g, post_ffn_g,
              hgrn_w_in, hgrn_lb_logits, hgrn_gnorm_g, hgrn_w_out,
              conv_w_in, conv_b_in, conv_dw_w, conv_dw_b, conv_ln_g, conv_ln_b,
              conv_w_out, conv_b_out, ffn_w_up, ffn_dw_w, ffn_dw_b, ffn_w_down):
    lb_all = jnp.cumsum(jax.nn.softmax(hgrn_lb_logits.astype(jnp.float32), axis=0), axis=0)
    cond = jax.nn.silu(c)
    for i in range(DEPTH):
        mod = (cond @ ada_w[i] + ada_b[i])[:, None, :]
        sh1, sc1, g1, sh2, sc2, g2 = jnp.split(mod, N_MOD, axis=-1)
        h = rmsnorm(x, pre_mix_g[i]) * (1.0 + sc1) + sh1
        j = i // N_MIXERS
        if i % N_MIXERS == 0:
            y = hgrn2_mixer(h, hgrn_w_in[j], lb_all[i], hgrn_gnorm_g[j], hgrn_w_out[j])
        else:
            y = conformer_conv_mixer(h, conv_w_in[j], conv_b_in[j], conv_dw_w[j], conv_dw_b[j],
                                     conv_ln_g[j], conv_ln_b[j], conv_w_out[j], conv_b_out[j])
        x = x + g1 * rmsnorm(y, post_mix_g[i])
        h = rmsnorm(x, pre_ffn_g[i]) * (1.0 + sc2) + sh2
        y = conv_ffn(h, ffn_w_up[i], ffn_dw_w[i], ffn_dw_b[i], ffn_w_down[i])
        x = x + g2 * rmsnorm(y, post_ffn_g[i])
    return x
```

```python
import functools

import jax
import jax.numpy as jnp
from jax import lax
from jax.experimental import pallas as pl
from jax.experimental.pallas import tpu as pltpu

F32 = jnp.float32
BF16 = jnp.bfloat16

D_MODEL = 1024
N_MOD = 6
HEADS = 8
HEAD_DIM = 128
D_FF = 2816
CONV_WIDTH = 31
FFN_CONV_WIDTH = 3
EPS = 1e-6

SUBLANES = 8
TOKENS_PER_STEP = 256
HGRN_CHUNK = 64
MAX_CHUNK_LOG_DECAY = 80.0
CONV_HALO = 32
FFN_COL_BLOCK = 256
MOD_COL_BLOCK = 1536
VMEM_LIMIT_BYTES = 56 * 1024 * 1024


def _sigmoid(x):
    return 1.0 / (1.0 + jnp.exp(-x))


def _silu(x):
    return x * _sigmoid(x)


def _rms(x, g):
    return x * lax.rsqrt(jnp.mean(x * x, axis=-1, keepdims=True) + EPS) * g


def _dot(a, b):
    return jnp.dot(a, b, preferred_element_type=F32)


def _dot_nt(a, b):
    return lax.dot_general(a, b, (((1,), (1,)), ((), ())), preferred_element_type=F32)


def _dot_tn(a, b):
    return lax.dot_general(a, b, (((0,), (0,)), ((), ())), preferred_element_type=F32)


def _mod_kernel(ct_ref, w_ref, b_ref, o_ref):
    cond = _silu(ct_ref[...])
    w = w_ref[0]
    for bi in range(cond.shape[1]):
        col = cond[:, bi:bi + 1]
        o_ref[0, bi:bi + 1, :] = jnp.sum(w * col, axis=0, keepdims=True) + b_ref[0]


def _modulation(c, ada_w, ada_b):
    depth, d, n = ada_w.shape
    bsz = c.shape[0]
    return pl.pallas_call(
        _mod_kernel,
        grid=(depth, n // MOD_COL_BLOCK),
        in_specs=[
            pl.BlockSpec((d, bsz), lambda i, j: (0, 0)),
            pl.BlockSpec((1, d, MOD_COL_BLOCK), lambda i, j: (i, 0, j)),
            pl.BlockSpec((1, 1, MOD_COL_BLOCK), lambda i, j: (i, 0, j)),
        ],
        out_specs=pl.BlockSpec((1, bsz, MOD_COL_BLOCK), lambda i, j: (i, 0, j)),
        out_shape=jax.ShapeDtypeStruct((depth, bsz, n), F32),
        compiler_params=pltpu.CompilerParams(
            dimension_semantics=("arbitrary", "arbitrary"), vmem_limit_bytes=VMEM_LIMIT_BYTES),
        name="adaln_modulation",
    )(c.T, ada_w, ada_b.reshape(depth, 1, n))


def _hgrn_kernel(x_ref, mod_ref, preg_ref, postg_ref, win_ref, lbl_ref, gn_ref, wout_ref, o_ref,
                 state_ref, q_s, k_s, b_s, v32_s, gate_s, qt_s, kt_s, vb_s, oh_s, *, layer):
    tm, d = x_ref.shape[1], x_ref.shape[2]
    c_len = HGRN_CHUNK
    n_chunks = tm // c_len

    @pl.when(pl.program_id(1) == 0)
    def _():
        state_ref[...] = jnp.zeros_like(state_ref)

    x = x_ref[0]
    mod = mod_ref[0]
    shift, scale = mod[0:1], mod[1:2]
    h = _rms(x, preg_ref[...]) * (1.0 + scale) + shift
    proj = _dot(h.astype(BF16), win_ref[...])

    logits = lbl_ref[...]
    ex = jnp.exp(logits - jnp.max(logits, axis=0, keepdims=True))
    lb = jnp.sum(ex[:layer + 1], axis=0, keepdims=True) / jnp.sum(ex, axis=0, keepdims=True)

    q = _silu(proj[:, :d])
    f = lb + (1.0 - lb) * _sigmoid(proj[:, d:2 * d])
    k = 1.0 - f
    logf = jnp.log(f)
    v = proj[:, 2 * d:3 * d]
    gate_s[...] = _silu(proj[:, 3 * d:])

    row = lax.broadcasted_iota(jnp.int32, (tm, tm), 0)
    col = lax.broadcasted_iota(jnp.int32, (tm, tm), 1)
    tri = jnp.where((col <= row) & (row // c_len == col // c_len), 1.0, 0.0).astype(BF16)
    hi = logf.astype(BF16)
    rest = logf - hi.astype(F32)
    mid = rest.astype(BF16)
    lo = (rest - mid.astype(F32)).astype(BF16)
    b = _dot(tri, hi) + _dot(tri, mid) + _dot(tri, lo)

    q_s[...] = q
    k_s[...] = k
    b_s[...] = b
    v32_s[...] = v
    vb_s[...] = v.astype(BF16)
    qt_s[...] = (q * jnp.exp(b)).astype(BF16)
    bounded = jnp.max(-b) <= MAX_CHUNK_LOG_DECAY

    t_idx = lax.broadcasted_iota(jnp.int32, (c_len, c_len), 0)
    s_idx = lax.broadcasted_iota(jnp.int32, (c_len, c_len), 1)
    causal = s_idx <= t_idx
    row_idx = lax.broadcasted_iota(jnp.int32, (c_len, HEAD_DIM), 0)
    sub_idx = lax.broadcasted_iota(jnp.int32, (SUBLANES, HEAD_DIM), 0)

    @pl.when(bounded)
    def _factorised():
        kt_s[...] = (k * jnp.exp(-b)).astype(BF16)

        def chunk_body(c, carry):
            r0 = pl.multiple_of(c * c_len, c_len)
            rows = pl.ds(r0, c_len)
            for hh in range(HEADS):
                cs = slice(hh * HEAD_DIM, (hh + 1) * HEAD_DIM)
                qt, kt, vb = qt_s[rows, cs], kt_s[rows, cs], vb_s[rows, cs]
                scores = jnp.where(causal, _dot_nt(qt, kt), 0.0).astype(BF16)
                st = state_ref[hh]
                oh_s[rows, cs] = _dot(scores, vb) + _dot_nt(qt, st.astype(BF16))
                b_end = b_s[pl.ds(r0 + c_len - SUBLANES, SUBLANES), cs][SUBLANES - 1:, :]
                state_ref[hh] = (st + _dot_tn(vb, kt)) * jnp.exp(b_end)
            return carry

        lax.fori_loop(0, n_chunks, chunk_body, 0)

    @pl.when(jnp.logical_not(bounded))
    def _pairwise():
        def chunk_body(c, carry):
            r0 = pl.multiple_of(c * c_len, c_len)
            rows = pl.ds(r0, c_len)
            for hh in range(HEADS):
                cs = slice(hh * HEAD_DIM, (hh + 1) * HEAD_DIM)
                qh, kh, bh = q_s[rows, cs], k_s[rows, cs], b_s[rows, cs]
                qt, vb = qt_s[rows, cs], vb_s[rows, cs]
                b_end = bh[c_len - 1:c_len, :]
                st = state_ref[hh]

                def s_body(s, o_acc):
                    grp = pl.ds(pl.multiple_of(r0 + (s // SUBLANES) * SUBLANES, SUBLANES), SUBLANES)
                    pick = sub_idx == s % SUBLANES
                    k_row = jnp.sum(jnp.where(pick, k_s[grp, cs], 0.0), axis=0, keepdims=True)
                    b_row = jnp.sum(jnp.where(pick, b_s[grp, cs], 0.0), axis=0, keepdims=True)
                    v_row = jnp.sum(jnp.where(pick, v32_s[grp, cs], 0.0), axis=0, keepdims=True)
                    decay = jnp.exp(jnp.minimum(bh - b_row, 0.0))
                    p = jnp.where(row_idx >= s, qh * decay * k_row, 0.0)
                    return o_acc + jnp.sum(p, axis=-1, keepdims=True) * v_row

                oh_s[rows, cs] = lax.fori_loop(0, c_len, s_body, _dot_nt(qt, st.astype(BF16)))
                k_hat = (kh * jnp.exp(b_end - bh)).astype(BF16)
                state_ref[hh] = st * jnp.exp(b_end) + _dot_tn(vb, k_hat)
            return carry

        lax.fori_loop(0, n_chunks, chunk_body, 0)

    gn = gn_ref[...]
    heads_out = []
    for hh in range(HEADS):
        cs = slice(hh * HEAD_DIM, (hh + 1) * HEAD_DIM)
        heads_out.append(_rms(oh_s[:, cs], gn))
    o = jnp.concatenate(heads_out, axis=-1) * gate_s[...]
    y = _dot(o.astype(BF16), wout_ref[...])
    o_ref[0] = x_ref[0] + mod_ref[0][2:3] * _rms(y, postg_ref[...])


def _hgrn_sublayer(x, mod, pre_g, post_g, w_in, lb_logits, gnorm_g, w_out, *, layer):
    bsz, t, d = x.shape
    tm = TOKENS_PER_STEP
    const = lambda *shape: pl.BlockSpec(shape, lambda bi, ti: (0,) * len(shape))
    tile = pl.BlockSpec((1, tm, d), lambda bi, ti: (bi, ti, 0))
    return pl.pallas_call(
        functools.partial(_hgrn_kernel, layer=layer),
        grid=(bsz, t // tm),
        in_specs=[
            tile,
            pl.BlockSpec((1, N_MOD, d), lambda bi, ti: (bi, 0, 0)),
            const(1, d), const(1, d),
            const(d, 4 * d),
            const(lb_logits.shape[0], d),
            const(1, HEAD_DIM),
            const(d, d),
        ],
        out_specs=tile,
        out_shape=jax.ShapeDtypeStruct(x.shape, F32),
        scratch_shapes=[
            pltpu.VMEM((HEADS, HEAD_DIM, HEAD_DIM), F32),
            pltpu.VMEM((tm, d), F32),
            pltpu.VMEM((tm, d), F32),
            pltpu.VMEM((tm, d), F32),
            pltpu.VMEM((tm, d), F32),
            pltpu.VMEM((tm, d), F32),
            pltpu.VMEM((tm, d), BF16),
            pltpu.VMEM((tm, d), BF16),
            pltpu.VMEM((tm, d), BF16),
            pltpu.VMEM((tm, d), F32),
        ],
        compiler_params=pltpu.CompilerParams(
            dimension_semantics=("arbitrary", "arbitrary"), vmem_limit_bytes=VMEM_LIMIT_BYTES),
        name=f"hgrn_sublayer_{layer}",
    )(x, mod, pre_g.reshape(1, d), post_g.reshape(1, d), w_in.astype(BF16), lb_logits,
      gnorm_g.reshape(1, HEAD_DIM), w_out.astype(BF16))


def _conformer_kernel(x_ref, mod_ref, preg_ref, postg_ref, win_ref, bin_ref, dww_ref, dwb_ref,
                      lng_ref, lnb_ref, wout_ref, bout_ref, o_ref, ext_ref, conv_s):
    tm, d = x_ref.shape[1], x_ref.shape[2]
    halo = CONV_HALO

    @pl.when(pl.program_id(1) == 0)
    def _():
        ext_ref[0:halo, :] = jnp.zeros((halo, d), F32)

    x = x_ref[0]
    mod = mod_ref[0]
    h = _rms(x, preg_ref[...]) * (1.0 + mod[1:2]) + mod[0:1]
    u = _dot(h.astype(BF16), win_ref[...]) + bin_ref[...]
    ext_ref[halo:halo + tm, :] = u[:, :d] * _sigmoid(u[:, d:])

    rb = 32
    first = halo - (CONV_WIDTH - 1)

    def row_block(i, carry):
        r0 = pl.multiple_of(i * rb, rb)
        acc = jnp.broadcast_to(dwb_ref[...], (rb, d))
        window = ext_ref[pl.ds(r0, rb + halo), :]
        for kk in range(CONV_WIDTH):
            acc = acc + dww_ref[kk:kk + 1, :] * window[first + kk:first + kk + rb, :]
        conv_s[pl.ds(r0, rb), :] = acc
        return carry

    lax.fori_loop(0, tm // rb, row_block, 0)
    ext_ref[0:halo, :] = ext_ref[tm:tm + halo, :]

    c = conv_s[...]
    mu = jnp.mean(c, axis=-1, keepdims=True)
    cc = c - mu
    var = jnp.mean(cc * cc, axis=-1, keepdims=True)
    ln = cc * lax.rsqrt(var + EPS) * lng_ref[...] + lnb_ref[...]
    y = _dot(_silu(ln).astype(BF16), wout_ref[...]) + bout_ref[...]
    o_ref[0] = x + mod[2:3] * _rms(y, postg_ref[...])


def _conformer_sublayer(x, mod, pre_g, post_g, w_in, b_in, dw_w, dw_b, ln_g, ln_b, w_out, b_out):
    bsz, t, d = x.shape
    tm = TOKENS_PER_STEP
    const = lambda *shape: pl.BlockSpec(shape, lambda bi, ti: (0,) * len(shape))
    tile = pl.BlockSpec((1, tm, d), lambda bi, ti: (bi, ti, 0))
    return pl.pallas_call(
        _conformer_kernel,
        grid=(bsz, t // tm),
        in_specs=[
            tile,
            pl.BlockSpec((1, N_MOD, d), lambda bi, ti: (bi, 0, 0)),
            const(1, d), const(1, d),
            const(d, 2 * d), const(1, 2 * d),
            const(CONV_WIDTH, d), const(1, d),
            const(1, d), const(1, d),
            const(d, d), const(1, d),
        ],
        out_specs=tile,
        out_shape=jax.ShapeDtypeStruct(x.shape, F32),
        scratch_shapes=[
            pltpu.VMEM((CONV_HALO + tm, d), F32),
            pltpu.VMEM((tm, d), F32),
        ],
        compiler_params=pltpu.CompilerParams(
            dimension_semantics=("arbitrary", "arbitrary"), vmem_limit_bytes=VMEM_LIMIT_BYTES),
        name="conformer_sublayer",
    )(x, mod, pre_g.reshape(1, d), post_g.reshape(1, d), w_in.astype(BF16), b_in.reshape(1, 2 * d),
      dw_w, dw_b.reshape(1, d), ln_g.reshape(1, d), ln_b.reshape(1, d), w_out.astype(BF16),
      b_out.reshape(1, d))


def _ffn_kernel(x_ref, mod_ref, preg_ref, postg_ref, wup_ref, dww_ref, dwb_ref, wdown_ref, o_ref,
                hist_ref):
    tm, d = x_ref.shape[1], x_ref.shape[2]
    fc = FFN_COL_BLOCK

    @pl.when(pl.program_id(1) == 0)
    def _():
        hist_ref[...] = jnp.zeros_like(hist_ref)

    x = x_ref[0]
    mod = mod_ref[0]
    h = (_rms(x, preg_ref[...]) * (1.0 + mod[4:5]) + mod[3:4]).astype(BF16)

    def conv_block(c0):
        cols = slice(c0, c0 + fc)
        u = _dot(h, wup_ref[:, cols])
        ext = jnp.concatenate([hist_ref[:, cols], u], axis=0)
        hist_ref[:, cols] = u[tm - SUBLANES:, :]
        out = dww_ref[2:3, cols] * u + dwb_ref[:, cols]
        for lag in (1, 2):
            shifted = pltpu.roll(ext, shift=lag, axis=0)[SUBLANES:, :]
            out = out + dww_ref[2 - lag:3 - lag, cols] * shifted
        return out

    acc = jnp.zeros((tm, d), F32)
    for j in range(D_FF // fc):
        a = conv_block(j * fc)
        g = conv_block(D_FF + j * fc)
        acc = acc + _dot((_silu(a) * g).astype(BF16), wdown_ref[j * fc:(j + 1) * fc, :])
    o_ref[0] = x + mod[5:6] * _rms(acc, postg_ref[...])


def _ffn_sublayer(x, mod, pre_g, post_g, w_up, dw_w, dw_b, w_down):
    bsz, t, d = x.shape
    tm = TOKENS_PER_STEP
    const = lambda *shape: pl.BlockSpec(shape, lambda bi, ti: (0,) * len(shape))
    tile = pl.BlockSpec((1, tm, d), lambda bi, ti: (bi, ti, 0))
    return pl.pallas_call(
        _ffn_kernel,
        grid=(bsz, t // tm),
        in_specs=[
            tile,
            pl.BlockSpec((1, N_MOD, d), lambda bi, ti: (bi, 0, 0)),
            const(1, d), const(1, d),
            const(d, 2 * D_FF),
            const(FFN_CONV_WIDTH, 2 * D_FF), const(1, 2 * D_FF),
            const(D_FF, d),
        ],
        out_specs=tile,
        out_shape=jax.ShapeDtypeStruct(x.shape, F32),
        scratch_shapes=[pltpu.VMEM((SUBLANES, 2 * D_FF), F32)],
        compiler_params=pltpu.CompilerParams(
            dimension_semantics=("arbitrary", "arbitrary"), vmem_limit_bytes=VMEM_LIMIT_BYTES),
        name="convffn_sublayer",
    )(x, mod, pre_g.reshape(1, d), post_g.reshape(1, d), w_up.astype(BF16), dw_w,
      dw_b.reshape(1, 2 * D_FF), w_down.astype(BF16))


def kernel(x, c, ada_w, ada_b, pre_mix_g, post_mix_g, pre_ffn_g, post_ffn_g, hgrn_w_in, hgrn_lb_logits, hgrn_gnorm_g, hgrn_w_out, conv_w_in, conv_b_in, conv_dw_w, conv_dw_b, conv_ln_g, conv_ln_b, conv_w_out, conv_b_out, ffn_w_up, ffn_dw_w, ffn_dw_b, ffn_w_down):
    depth = ada_w.shape[0]
    bsz, _, d = x.shape
    mod = _modulation(c, ada_w, ada_b).reshape(depth, bsz, N_MOD, d)
    for i in range(depth):
        j = i // 2
        if i % 2 == 0:
            x = _hgrn_sublayer(x, mod[i], pre_mix_g[i], post_mix_g[i], hgrn_w_in[j], hgrn_lb_logits,
                               hgrn_gnorm_g[j], hgrn_w_out[j], layer=i)
        else:
            x = _conformer_sublayer(x, mod[i], pre_mix_g[i], post_mix_g[i], conv_w_in[j], conv_b_in[j],
                                    conv_dw_w[j], conv_dw_b[j], conv_ln_g[j], conv_ln_b[j],
                                    conv_w_out[j], conv_b_out[j])
        x = _ffn_sublayer(x, mod[i], pre_ffn_g[i], post_ffn_g[i], ffn_w_up[i], ffn_dw_w[i],
                          ffn_dw_b[i], ffn_w_down[i])
    return x
```

```python
import functools

import jax
import jax.numpy as jnp
from jax import lax
from jax.experimental import pallas as pl
from jax.experimental.pallas import tpu as pltpu

F32 = jnp.float32
BF16 = jnp.bfloat16

D_MODEL = 1024
N_MOD = 6
HEADS = 8
HEAD_DIM = 128
D_FF = 2816
CONV_WIDTH = 31
FFN_CONV_WIDTH = 3
EPS = 1e-6

SUBLANES = 8
TOKENS_PER_STEP = 256
HGRN_CHUNK = 64
MAX_CHUNK_LOG_DECAY = 160.0
CONV_HALO = 32
FFN_COL_BLOCK = 256
MOD_COL_BLOCK = 1536
VMEM_LIMIT_BYTES = 56 * 1024 * 1024


def _sigmoid(x):
    return 1.0 / (1.0 + jnp.exp(-x))


def _silu(x):
    return x * _sigmoid(x)


def _rms(x, g):
    return x * lax.rsqrt(jnp.mean(x * x, axis=-1, keepdims=True) + EPS) * g


def _dot(a, b):
    return jnp.dot(a, b, preferred_element_type=F32)


def _dot_nt(a, b):
    return lax.dot_general(a, b, (((1,), (1,)), ((), ())), preferred_element_type=F32)


def _dot_tn(a, b):
    return lax.dot_general(a, b, (((0,), (0,)), ((), ())), preferred_element_type=F32)


def _mod_kernel(ct_ref, w_ref, b_ref, o_ref):
    cond = _silu(ct_ref[...])
    w = w_ref[0]
    for bi in range(cond.shape[1]):
        col = cond[:, bi:bi + 1]
        o_ref[0, bi:bi + 1, :] = jnp.sum(w * col, axis=0, keepdims=True) + b_ref[0]


def _modulation(c, ada_w, ada_b):
    depth, d, n = ada_w.shape
    bsz = c.shape[0]
    return pl.pallas_call(
        _mod_kernel,
        grid=(depth, n // MOD_COL_BLOCK),
        in_specs=[
            pl.BlockSpec((d, bsz), lambda i, j: (0, 0)),
            pl.BlockSpec((1, d, MOD_COL_BLOCK), lambda i, j: (i, 0, j)),
            pl.BlockSpec((1, 1, MOD_COL_BLOCK), lambda i, j: (i, 0, j)),
        ],
        out_specs=pl.BlockSpec((1, bsz, MOD_COL_BLOCK), lambda i, j: (i, 0, j)),
        out_shape=jax.ShapeDtypeStruct((depth, bsz, n), F32),
        compiler_params=pltpu.CompilerParams(
            dimension_semantics=("arbitrary", "arbitrary"), vmem_limit_bytes=VMEM_LIMIT_BYTES),
        name="adaln_modulation",
    )(c.T, ada_w, ada_b.reshape(depth, 1, n))


def _hgrn_kernel(x_ref, mod_ref, preg_ref, postg_ref, win_ref, lbl_ref, gn_ref, wout_ref, o_ref,
                 state_ref, q_s, k_s, b_s, v32_s, gate_s, vb_s, oh_s, *, layer):
    tm, d = x_ref.shape[1], x_ref.shape[2]
    c_len = HGRN_CHUNK
    n_chunks = tm // c_len

    @pl.when(pl.program_id(1) == 0)
    def _():
        state_ref[...] = jnp.zeros_like(state_ref)

    x = x_ref[0]
    mod = mod_ref[0]
    shift, scale = mod[0:1], mod[1:2]
    h = _rms(x, preg_ref[...]) * (1.0 + scale) + shift
    proj = _dot(h.astype(BF16), win_ref[...])

    logits = lbl_ref[...]
    ex = jnp.exp(logits - jnp.max(logits, axis=0, keepdims=True))
    lb = jnp.sum(ex[:layer + 1], axis=0, keepdims=True) / jnp.sum(ex, axis=0, keepdims=True)

    q = _silu(proj[:, :d])
    f = lb + (1.0 - lb) * _sigmoid(proj[:, d:2 * d])
    k = 1.0 - f
    logf = jnp.log(f)
    v = proj[:, 2 * d:3 * d]
    gate_s[...] = _silu(proj[:, 3 * d:])

    row = lax.broadcasted_iota(jnp.int32, (tm, tm), 0)
    col = lax.broadcasted_iota(jnp.int32, (tm, tm), 1)
    tri = jnp.where((col <= row) & (row // c_len == col // c_len), 1.0, 0.0).astype(BF16)
    hi = logf.astype(BF16)
    rest = logf - hi.astype(F32)
    mid = rest.astype(BF16)
    lo = (rest - mid.astype(F32)).astype(BF16)
    b = _dot(tri, hi) + _dot(tri, mid) + _dot(tri, lo)

    q_s[...] = q
    k_s[...] = k
    b_s[...] = b
    v32_s[...] = v
    vb_s[...] = v.astype(BF16)
    bounded = jnp.max(-b) <= MAX_CHUNK_LOG_DECAY

    t_idx = lax.broadcasted_iota(jnp.int32, (c_len, c_len), 0)
    s_idx = lax.broadcasted_iota(jnp.int32, (c_len, c_len), 1)
    causal = s_idx <= t_idx
    row_idx = lax.broadcasted_iota(jnp.int32, (c_len, HEAD_DIM), 0)
    sub_idx = lax.broadcasted_iota(jnp.int32, (SUBLANES, HEAD_DIM), 0)

    @pl.when(bounded)
    def _factorised():
        def chunk_body(c, carry):
            r0 = pl.multiple_of(c * c_len, c_len)
            rows = pl.ds(r0, c_len)
            for hh in range(HEADS):
                cs = slice(hh * HEAD_DIM, (hh + 1) * HEAD_DIM)
                qh, kh, bh, vb = q_s[rows, cs], k_s[rows, cs], b_s[rows, cs], vb_s[rows, cs]
                b_end = bh[c_len - 1:c_len, :]
                anchor = 0.5 * b_end
                e_anchor = jnp.exp(anchor)
                qt = (qh * jnp.exp(bh - anchor)).astype(BF16)
                kt = (kh * jnp.exp(anchor - bh)).astype(BF16)
                scores = jnp.where(causal, _dot_nt(qt, kt), 0.0).astype(BF16)
                st = state_ref[hh]
                oh_s[rows, cs] = _dot(scores, vb) + _dot_nt(qt, (st * e_anchor).astype(BF16))
                state_ref[hh] = st * jnp.exp(b_end) + _dot_tn(vb, kt) * e_anchor
            return carry

        lax.fori_loop(0, n_chunks, chunk_body, 0)

    @pl.when(jnp.logical_not(bounded))
    def _pairwise():
        def chunk_body(c, carry):
            r0 = pl.multiple_of(c * c_len, c_len)
            rows = pl.ds(r0, c_len)
            for hh in range(HEADS):
                cs = slice(hh * HEAD_DIM, (hh + 1) * HEAD_DIM)
                qh, kh, bh = q_s[rows, cs], k_s[rows, cs], b_s[rows, cs]
                qt, vb = (qh * jnp.exp(bh)).astype(BF16), vb_s[rows, cs]
                b_end = bh[c_len - 1:c_len, :]
                st = state_ref[hh]

                def s_body(s, o_acc):
                    grp = pl.ds(pl.multiple_of(r0 + (s // SUBLANES) * SUBLANES, SUBLANES), SUBLANES)
                    pick = sub_idx == s % SUBLANES
                    k_row = jnp.sum(jnp.where(pick, k_s[grp, cs], 0.0), axis=0, keepdims=True)
                    b_row = jnp.sum(jnp.where(pick, b_s[grp, cs], 0.0), axis=0, keepdims=True)
                    v_row = jnp.sum(jnp.where(pick, v32_s[grp, cs], 0.0), axis=0, keepdims=True)
                    decay = jnp.exp(jnp.minimum(bh - b_row, 0.0))
                    p = jnp.where(row_idx >= s, qh * decay * k_row, 0.0)
                    return o_acc + jnp.sum(p, axis=-1, keepdims=True) * v_row

                oh_s[rows, cs] = lax.fori_loop(0, c_len, s_body, _dot_nt(qt, st.astype(BF16)))
                k_hat = (kh * jnp.exp(b_end - bh)).astype(BF16)
                state_ref[hh] = st * jnp.exp(b_end) + _dot_tn(vb, k_hat)
            return carry

        lax.fori_loop(0, n_chunks, chunk_body, 0)

    gn = gn_ref[...]
    heads_out = []
    for hh in range(HEADS):
        cs = slice(hh * HEAD_DIM, (hh + 1) * HEAD_DIM)
        heads_out.append(_rms(oh_s[:, cs], gn))
    o = jnp.concatenate(heads_out, axis=-1) * gate_s[...]
    y = _dot(o.astype(BF16), wout_ref[...])
    o_ref[0] = x_ref[0] + mod_ref[0][2:3] * _rms(y, postg_ref[...])


def _hgrn_sublayer(x, mod, pre_g, post_g, w_in, lb_logits, gnorm_g, w_out, *, layer):
    bsz, t, d = x.shape
    tm = TOKENS_PER_STEP
    const = lambda *shape: pl.BlockSpec(shape, lambda bi, ti: (0,) * len(shape))
    tile = pl.BlockSpec((1, tm, d), lambda bi, ti: (bi, ti, 0))
    return pl.pallas_call(
        functools.partial(_hgrn_kernel, layer=layer),
        grid=(bsz, t // tm),
        in_specs=[
            tile,
            pl.BlockSpec((1, N_MOD, d), lambda bi, ti: (bi, 0, 0)),
            const(1, d), const(1, d),
            const(d, 4 * d),
            const(lb_logits.shape[0], d),
            const(1, HEAD_DIM),
            const(d, d),
        ],
        out_specs=tile,
        out_shape=jax.ShapeDtypeStruct(x.shape, F32),
        scratch_shapes=[
            pltpu.VMEM((HEADS, HEAD_DIM, HEAD_DIM), F32),
            pltpu.VMEM((tm, d), F32),
            pltpu.VMEM((tm, d), F32),
            pltpu.VMEM((tm, d), F32),
            pltpu.VMEM((tm, d), F32),
            pltpu.VMEM((tm, d), F32),
            pltpu.VMEM((tm, d), BF16),
            pltpu.VMEM((tm, d), F32),
        ],
        compiler_params=pltpu.CompilerParams(
            dimension_semantics=("arbitrary", "arbitrary"), vmem_limit_bytes=VMEM_LIMIT_BYTES),
        name=f"hgrn_sublayer_{layer}",
    )(x, mod, pre_g.reshape(1, d), post_g.reshape(1, d), w_in.astype(BF16), lb_logits,
      gnorm_g.reshape(1, HEAD_DIM), w_out.astype(BF16))


def _conformer_kernel(x_ref, mod_ref, preg_ref, postg_ref, win_ref, bin_ref, dww_ref, dwb_ref,
                      lng_ref, lnb_ref, wout_ref, bout_ref, o_ref, ext_ref, conv_s):
    tm, d = x_ref.shape[1], x_ref.shape[2]
    halo = CONV_HALO

    @pl.when(pl.program_id(1) == 0)
    def _():
        ext_ref[0:halo, :] = jnp.zeros((halo, d), F32)

    x = x_ref[0]
    mod = mod_ref[0]
    h = _rms(x, preg_ref[...]) * (1.0 + mod[1:2]) + mod[0:1]
    u = _dot(h.astype(BF16), win_ref[...]) + bin_ref[...]
    ext_ref[halo:halo + tm, :] = u[:, :d] * _sigmoid(u[:, d:])

    rb = 32
    first = halo - (CONV_WIDTH - 1)

    def row_block(i, carry):
        r0 = pl.multiple_of(i * rb, rb)
        acc = jnp.broadcast_to(dwb_ref[...], (rb, d))
        window = ext_ref[pl.ds(r0, rb + halo), :]
        for kk in range(CONV_WIDTH):
            acc = acc + dww_ref[kk:kk + 1, :] * window[first + kk:first + kk + rb, :]
        conv_s[pl.ds(r0, rb), :] = acc
        return carry

    lax.fori_loop(0, tm // rb, row_block, 0)
    ext_ref[0:halo, :] = ext_ref[tm:tm + halo, :]

    c = conv_s[...]
    mu = jnp.mean(c, axis=-1, keepdims=True)
    cc = c - mu
    var = jnp.mean(cc * cc, axis=-1, keepdims=True)
    ln = cc * lax.rsqrt(var + EPS) * lng_ref[...] + lnb_ref[...]
    y = _dot(_silu(ln).astype(BF16), wout_ref[...]) + bout_ref[...]
    o_ref[0] = x + mod[2:3] * _rms(y, postg_ref[...])


def _conformer_sublayer(x, mod, pre_g, post_g, w_in, b_in, dw_w, dw_b, ln_g, ln_b, w_out, b_out):
    bsz, t, d = x.shape
    tm = TOKENS_PER_STEP
    const = lambda *shape: pl.BlockSpec(shape, lambda bi, ti: (0,) * len(shape))
    tile = pl.BlockSpec((1, tm, d), lambda bi, ti: (bi, ti, 0))
    return pl.pallas_call(
        _conformer_kernel,
        grid=(bsz, t // tm),
        in_specs=[
            tile,
            pl.BlockSpec((1, N_MOD, d), lambda bi, ti: (bi, 0, 0)),
            const(1, d), const(1, d),
            const(d, 2 * d), const(1, 2 * d),
            const(CONV_WIDTH, d), const(1, d),
            const(1, d), const(1, d),
            const(d, d), const(1, d),
        ],
        out_specs=tile,
        out_shape=jax.ShapeDtypeStruct(x.shape, F32),
        scratch_shapes=[
            pltpu.VMEM((CONV_HALO + tm, d), F32),
            pltpu.VMEM((tm, d), F32),
        ],
        compiler_params=pltpu.CompilerParams(
            dimension_semantics=("arbitrary", "arbitrary"), vmem_limit_bytes=VMEM_LIMIT_BYTES),
        name="conformer_sublayer",
    )(x, mod, pre_g.reshape(1, d), post_g.reshape(1, d), w_in.astype(BF16), b_in.reshape(1, 2 * d),
      dw_w, dw_b.reshape(1, d), ln_g.reshape(1, d), ln_b.reshape(1, d), w_out.astype(BF16),
      b_out.reshape(1, d))


def _ffn_kernel(x_ref, mod_ref, preg_ref, postg_ref, wup_ref, dww_ref, dwb_ref, wdown_ref, o_ref,
                hist_ref, act_s):
    tm = x_ref.shape[1]
    fc = FFN_COL_BLOCK

    @pl.when(pl.program_id(1) == 0)
    def _():
        hist_ref[...] = jnp.zeros_like(hist_ref)

    x = x_ref[0]
    mod = mod_ref[0]
    h = (_rms(x, preg_ref[...]) * (1.0 + mod[4:5]) + mod[3:4]).astype(BF16)

    def conv_block(c0):
        cols = slice(c0, c0 + fc)
        u = _dot(h, wup_ref[:, cols])
        ext = jnp.concatenate([hist_ref[:, cols], u], axis=0)
        hist_ref[:, cols] = u[tm - SUBLANES:, :]
        out = dww_ref[2:3, cols] * u + dwb_ref[:, cols]
        for lag in (1, 2):
            shifted = pltpu.roll(ext, shift=lag, axis=0)[SUBLANES:, :]
            out = out + dww_ref[2 - lag:3 - lag, cols] * shifted
        return out

    for j in range(D_FF // fc):
        a = conv_block(j * fc)
        g = conv_block(D_FF + j * fc)
        act_s[:, j * fc:(j + 1) * fc] = (_silu(a) * g).astype(BF16)
    y = _dot(act_s[...], wdown_ref[...])
    o_ref[0] = x + mod[5:6] * _rms(y, postg_ref[...])


def _ffn_sublayer(x, mod, pre_g, post_g, w_up, dw_w, dw_b, w_down):
    bsz, t, d = x.shape
    tm = TOKENS_PER_STEP
    const = lambda *shape: pl.BlockSpec(shape, lambda bi, ti: (0,) * len(shape))
    tile = pl.BlockSpec((1, tm, d), lambda bi, ti: (bi, ti, 0))
    return pl.pallas_call(
        _ffn_kernel,
        grid=(bsz, t // tm),
        in_specs=[
            tile,
            pl.BlockSpec((1, N_MOD, d), lambda bi, ti: (bi, 0, 0)),
            const(1, d), const(1, d),
            const(d, 2 * D_FF),
            const(FFN_CONV_WIDTH, 2 * D_FF), const(1, 2 * D_FF),
            const(D_FF, d),
        ],
        out_specs=tile,
        out_shape=jax.ShapeDtypeStruct(x.shape, F32),
        scratch_shapes=[
            pltpu.VMEM((SUBLANES, 2 * D_FF), F32),
            pltpu.VMEM((tm, D_FF), BF16),
        ],
        compiler_params=pltpu.CompilerParams(
            dimension_semantics=("arbitrary", "arbitrary"), vmem_limit_bytes=VMEM_LIMIT_BYTES),
        name="convffn_sublayer",
    )(x, mod, pre_g.reshape(1, d), post_g.reshape(1, d), w_up.astype(BF16), dw_w,
      dw_b.reshape(1, 2 * D_FF), w_down.astype(BF16))


def kernel(x, c, ada_w, ada_b, pre_mix_g, post_mix_g, pre_ffn_g, post_ffn_g, hgrn_w_in, hgrn_lb_logits, hgrn_gnorm_g, hgrn_w_out, conv_w_in, conv_b_in, conv_dw_w, conv_dw_b, conv_ln_g, conv_ln_b, conv_w_out, conv_b_out, ffn_w_up, ffn_dw_w, ffn_dw_b, ffn_w_down):
    depth = ada_w.shape[0]
    bsz, _, d = x.shape
    mod = _modulation(c, ada_w, ada_b).reshape(depth, bsz, N_MOD, d)
    for i in range(depth):
        j = i // 2
        if i % 2 == 0:
            x = _hgrn_sublayer(x, mod[i], pre_mix_g[i], post_mix_g[i], hgrn_w_in[j], hgrn_lb_logits,
                               hgrn_gnorm_g[j], hgrn_w_out[j], layer=i)
        else:
            x = _conformer_sublayer(x, mod[i], pre_mix_g[i], post_mix_g[i], conv_w_in[j], conv_b_in[j],
                                    conv_dw_w[j], conv_dw_b[j], conv_ln_g[j], conv_ln_b[j],
                                    conv_w_out[j], conv_b_out[j])
        x = _ffn_sublayer(x, mod[i], pre_ffn_g[i], post_ffn_g[i], ffn_w_up[i], ffn_dw_w[i],
                          ffn_dw_b[i], ffn_w_down[i])
    return x
```

```python
import functools

import jax
import jax.numpy as jnp
from jax import lax
from jax.experimental import pallas as pl
from jax.experimental.pallas import tpu as pltpu

F32 = jnp.float32
BF16 = jnp.bfloat16

D_MODEL = 1024
N_MOD = 6
HEADS = 8
HEAD_DIM = 128
D_FF = 2816
CONV_WIDTH = 31
FFN_CONV_WIDTH = 3
EPS = 1e-6

SUBLANES = 8
TOKENS_PER_STEP = 512
HGRN_CHUNK = 64
CUMSUM_BLOCK = 256
MAX_CHUNK_LOG_DECAY = 160.0
CONV_HALO = 32
CONV_ROW_BLOCK = 64
CONV_COL_BLOCK = 256
FFN_COL_BLOCK = 256
MOD_COL_BLOCK = 1536
VMEM_LIMIT_BYTES = 56 * 1024 * 1024


def _sigmoid(x):
    return 1.0 / (1.0 + jnp.exp(-x))


def _silu(x):
    return x * _sigmoid(x)


def _rms(x, g):
    return x * lax.rsqrt(jnp.mean(x * x, axis=-1, keepdims=True) + EPS) * g


def _dot(a, b):
    return jnp.dot(a, b, preferred_element_type=F32)


def _dot_nt(a, b):
    return lax.dot_general(a, b, (((1,), (1,)), ((), ())), preferred_element_type=F32)


def _dot_tn(a, b):
    return lax.dot_general(a, b, (((0,), (0,)), ((), ())), preferred_element_type=F32)


def _resident(*shape):
    return pl.BlockSpec(shape, lambda bi, ti: (0,) * len(shape), pipeline_mode=pl.Buffered(1))


def _mod_kernel(ct_ref, w_ref, b_ref, o_ref):
    cond = _silu(ct_ref[...])
    w = w_ref[0]
    for bi in range(cond.shape[1]):
        col = cond[:, bi:bi + 1]
        o_ref[0, bi:bi + 1, :] = jnp.sum(w * col, axis=0, keepdims=True) + b_ref[0]


def _modulation(c, ada_w, ada_b):
    depth, d, n = ada_w.shape
    bsz = c.shape[0]
    return pl.pallas_call(
        _mod_kernel,
        grid=(depth, n // MOD_COL_BLOCK),
        in_specs=[
            pl.BlockSpec((d, bsz), lambda i, j: (0, 0)),
            pl.BlockSpec((1, d, MOD_COL_BLOCK), lambda i, j: (i, 0, j)),
            pl.BlockSpec((1, 1, MOD_COL_BLOCK), lambda i, j: (i, 0, j)),
        ],
        out_specs=pl.BlockSpec((1, bsz, MOD_COL_BLOCK), lambda i, j: (i, 0, j)),
        out_shape=jax.ShapeDtypeStruct((depth, bsz, n), F32),
        compiler_params=pltpu.CompilerParams(
            dimension_semantics=("arbitrary", "arbitrary"), vmem_limit_bytes=VMEM_LIMIT_BYTES),
        name="adaln_modulation",
    )(c.T, ada_w, ada_b.reshape(depth, 1, n))


def _hgrn_kernel(x_ref, mod_ref, preg_ref, postg_ref, win_ref, lbl_ref, gn_ref, wout_ref, o_ref,
                 state_ref, q_s, k_s, b_s, v32_s, gate_s, vb_s, oh_s, *, layer):
    tm, d = x_ref.shape[1], x_ref.shape[2]
    c_len = HGRN_CHUNK
    n_chunks = tm // c_len

    @pl.when(pl.program_id(1) == 0)
    def _():
        state_ref[...] = jnp.zeros_like(state_ref)

    x = x_ref[0]
    mod = mod_ref[0]
    shift, scale = mod[0:1], mod[1:2]
    h = _rms(x, preg_ref[...]) * (1.0 + scale) + shift
    proj = _dot(h.astype(BF16), win_ref[...])

    logits = lbl_ref[...]
    ex = jnp.exp(logits - jnp.max(logits, axis=0, keepdims=True))
    lb = jnp.sum(ex[:layer + 1], axis=0, keepdims=True) / jnp.sum(ex, axis=0, keepdims=True)

    q = _silu(proj[:, :d])
    f = lb + (1.0 - lb) * _sigmoid(proj[:, d:2 * d])
    k = 1.0 - f
    logf = jnp.log(f)
    v = proj[:, 2 * d:3 * d]
    gate_s[...] = _silu(proj[:, 3 * d:])

    blk = CUMSUM_BLOCK
    row = lax.broadcasted_iota(jnp.int32, (blk, blk), 0)
    col = lax.broadcasted_iota(jnp.int32, (blk, blk), 1)
    tri = jnp.where((col <= row) & (row // c_len == col // c_len), 1.0, 0.0).astype(BF16)
    hi = logf.astype(BF16)
    rest = logf - hi.astype(F32)
    mid = rest.astype(BF16)
    lo = (rest - mid.astype(F32)).astype(BF16)
    max_decay = None
    for r in range(tm // blk):
        rows = slice(r * blk, (r + 1) * blk)
        b = _dot(tri, hi[rows]) + _dot(tri, mid[rows]) + _dot(tri, lo[rows])
        b_s[rows, :] = b
        block_max = jnp.max(-b)
        max_decay = block_max if max_decay is None else jnp.maximum(max_decay, block_max)

    q_s[...] = q
    k_s[...] = k
    v32_s[...] = v
    vb_s[...] = v.astype(BF16)
    bounded = max_decay <= MAX_CHUNK_LOG_DECAY

    causal = (lax.broadcasted_iota(jnp.int32, (c_len, c_len), 1)
              <= lax.broadcasted_iota(jnp.int32, (c_len, c_len), 0))
    row_idx = lax.broadcasted_iota(jnp.int32, (c_len, HEAD_DIM), 0)
    sub_idx = lax.broadcasted_iota(jnp.int32, (SUBLANES, HEAD_DIM), 0)

    @pl.when(bounded)
    def _factorised():
        for hh in range(HEADS):
            cs = slice(hh * HEAD_DIM, (hh + 1) * HEAD_DIM)
            st = state_ref[hh]
            for c in range(n_chunks):
                rows = slice(c * c_len, (c + 1) * c_len)
                qh, kh, bh, vb = q_s[rows, cs], k_s[rows, cs], b_s[rows, cs], vb_s[rows, cs]
                b_end = bh[c_len - 1:c_len, :]
                anchor = 0.5 * b_end
                e_anchor = jnp.exp(anchor)
                qt = (qh * jnp.exp(bh - anchor)).astype(BF16)
                kt = (kh * jnp.exp(anchor - bh)).astype(BF16)
                scores = jnp.where(causal, _dot_nt(qt, kt), 0.0).astype(BF16)
                oh_s[rows, cs] = _dot(scores, vb) + _dot_nt(qt, (st * e_anchor).astype(BF16))
                st = st * jnp.exp(b_end) + _dot_tn(vb, kt) * e_anchor
            state_ref[hh] = st

    @pl.when(jnp.logical_not(bounded))
    def _pairwise():
        def chunk_body(c, carry):
            r0 = pl.multiple_of(c * c_len, c_len)
            rows = pl.ds(r0, c_len)
            for hh in range(HEADS):
                cs = slice(hh * HEAD_DIM, (hh + 1) * HEAD_DIM)
                qh, kh, bh = q_s[rows, cs], k_s[rows, cs], b_s[rows, cs]
                qt, vb = (qh * jnp.exp(bh)).astype(BF16), vb_s[rows, cs]
                b_end = bh[c_len - 1:c_len, :]
                st = state_ref[hh]

                def s_body(s, o_acc):
                    grp = pl.ds(pl.multiple_of(r0 + (s // SUBLANES) * SUBLANES, SUBLANES), SUBLANES)
                    pick = sub_idx == s % SUBLANES
                    k_row = jnp.sum(jnp.where(pick, k_s[grp, cs], 0.0), axis=0, keepdims=True)
                    b_row = jnp.sum(jnp.where(pick, b_s[grp, cs], 0.0), axis=0, keepdims=True)
                    v_row = jnp.sum(jnp.where(pick, v32_s[grp, cs], 0.0), axis=0, keepdims=True)
                    decay = jnp.exp(jnp.minimum(bh - b_row, 0.0))
                    p = jnp.where(row_idx >= s, qh * decay * k_row, 0.0)
                    return o_acc + jnp.sum(p, axis=-1, keepdims=True) * v_row

                oh_s[rows, cs] = lax.fori_loop(0, c_len, s_body, _dot_nt(qt, st.astype(BF16)))
                k_hat = (kh * jnp.exp(b_end - bh)).astype(BF16)
                state_ref[hh] = st * jnp.exp(b_end) + _dot_tn(vb, k_hat)
            return carry

        lax.fori_loop(0, n_chunks, chunk_body, 0)

    gn = gn_ref[...]
    heads_out = []
    for hh in range(HEADS):
        cs = slice(hh * HEAD_DIM, (hh + 1) * HEAD_DIM)
        heads_out.append(_rms(oh_s[:, cs], gn))
    o = jnp.concatenate(heads_out, axis=-1) * gate_s[...]
    y = _dot(o.astype(BF16), wout_ref[...])
    o_ref[0] = x_ref[0] + mod_ref[0][2:3] * _rms(y, postg_ref[...])


def _hgrn_sublayer(x, mod, pre_g, post_g, w_in, lb_logits, gnorm_g, w_out, *, layer):
    bsz, t, d = x.shape
    tm = TOKENS_PER_STEP
    const = _resident
    tile = pl.BlockSpec((1, tm, d), lambda bi, ti: (bi, ti, 0))
    return pl.pallas_call(
        functools.partial(_hgrn_kernel, layer=layer),
        grid=(bsz, t // tm),
        in_specs=[
            tile,
            pl.BlockSpec((1, N_MOD, d), lambda bi, ti: (bi, 0, 0)),
            const(1, d), const(1, d),
            const(d, 4 * d),
            const(lb_logits.shape[0], d),
            const(1, HEAD_DIM),
            const(d, d),
        ],
        out_specs=tile,
        out_shape=jax.ShapeDtypeStruct(x.shape, F32),
        scratch_shapes=[
            pltpu.VMEM((HEADS, HEAD_DIM, HEAD_DIM), F32),
            pltpu.VMEM((tm, d), F32),
            pltpu.VMEM((tm, d), F32),
            pltpu.VMEM((tm, d), F32),
            pltpu.VMEM((tm, d), F32),
            pltpu.VMEM((tm, d), F32),
            pltpu.VMEM((tm, d), BF16),
            pltpu.VMEM((tm, d), F32),
        ],
        compiler_params=pltpu.CompilerParams(
            dimension_semantics=("arbitrary", "arbitrary"), vmem_limit_bytes=VMEM_LIMIT_BYTES),
        name=f"hgrn_sublayer_{layer}",
    )(x, mod, pre_g.reshape(1, d), post_g.reshape(1, d), w_in.astype(BF16), lb_logits,
      gnorm_g.reshape(1, HEAD_DIM), w_out.astype(BF16))


def _conformer_kernel(x_ref, mod_ref, preg_ref, postg_ref, win_ref, bin_ref, dww_ref, dwb_ref,
                      lng_ref, lnb_ref, wout_ref, bout_ref, o_ref, ext_ref, conv_s):
    tm, d = x_ref.shape[1], x_ref.shape[2]
    halo = CONV_HALO

    @pl.when(pl.program_id(1) == 0)
    def _():
        ext_ref[0:halo, :] = jnp.zeros((halo, d), F32)

    x = x_ref[0]
    mod = mod_ref[0]
    h = _rms(x, preg_ref[...]) * (1.0 + mod[1:2]) + mod[0:1]
    u = _dot(h.astype(BF16), win_ref[...]) + bin_ref[...]
    ext_ref[halo:halo + tm, :] = u[:, :d] * _sigmoid(u[:, d:])

    rb, cw = CONV_ROW_BLOCK, CONV_COL_BLOCK
    first = halo - (CONV_WIDTH - 1)

    def row_block(i, carry):
        r0 = pl.multiple_of(i * rb, rb)
        for cb in range(d // cw):
            cols = slice(cb * cw, (cb + 1) * cw)
            y = jnp.broadcast_to(dwb_ref[:, cols], (rb, cw))
            for rr in range(SUBLANES):
                n = rb if rr == 0 else rb + SUBLANES
                z = None
                for m in range(halo // SUBLANES + 1):
                    kk = SUBLANES * m + rr - first
                    if 0 <= kk < CONV_WIDTH:
                        start = pl.multiple_of(r0 + SUBLANES * m, SUBLANES)
                        term = dww_ref[kk:kk + 1, cols] * ext_ref[pl.ds(start, n), cols]
                        z = term if z is None else z + term
                y = y + (z if rr == 0 else z[rr:rr + rb, :])
            conv_s[pl.ds(r0, rb), cols] = y
        return carry

    lax.fori_loop(0, tm // rb, row_block, 0)
    ext_ref[0:halo, :] = ext_ref[tm:tm + halo, :]

    c = conv_s[...]
    mu = jnp.mean(c, axis=-1, keepdims=True)
    cc = c - mu
    var = jnp.mean(cc * cc, axis=-1, keepdims=True)
    ln = cc * lax.rsqrt(var + EPS) * lng_ref[...] + lnb_ref[...]
    y = _dot(_silu(ln).astype(BF16), wout_ref[...]) + bout_ref[...]
    o_ref[0] = x + mod[2:3] * _rms(y, postg_ref[...])


def _conformer_sublayer(x, mod, pre_g, post_g, w_in, b_in, dw_w, dw_b, ln_g, ln_b, w_out, b_out):
    bsz, t, d = x.shape
    tm = TOKENS_PER_STEP
    const = _resident
    tile = pl.BlockSpec((1, tm, d), lambda bi, ti: (bi, ti, 0))
    return pl.pallas_call(
        _conformer_kernel,
        grid=(bsz, t // tm),
        in_specs=[
            tile,
            pl.BlockSpec((1, N_MOD, d), lambda bi, ti: (bi, 0, 0)),
            const(1, d), const(1, d),
            const(d, 2 * d), const(1, 2 * d),
            const(CONV_WIDTH, d), const(1, d),
            const(1, d), const(1, d),
            const(d, d), const(1, d),
        ],
        out_specs=tile,
        out_shape=jax.ShapeDtypeStruct(x.shape, F32),
        scratch_shapes=[
            pltpu.VMEM((CONV_HALO + tm, d), F32),
            pltpu.VMEM((tm, d), F32),
        ],
        compiler_params=pltpu.CompilerParams(
            dimension_semantics=("arbitrary", "arbitrary"), vmem_limit_bytes=VMEM_LIMIT_BYTES),
        name="conformer_sublayer",
    )(x, mod, pre_g.reshape(1, d), post_g.reshape(1, d), w_in.astype(BF16), b_in.reshape(1, 2 * d),
      dw_w, dw_b.reshape(1, d), ln_g.reshape(1, d), ln_b.reshape(1, d), w_out.astype(BF16),
      b_out.reshape(1, d))


def _ffn_kernel(x_ref, mod_ref, preg_ref, postg_ref, wup_ref, dww_ref, dwb_ref, wdown_ref, o_ref,
                hist_ref, act_s):
    tm = x_ref.shape[1]
    fc = FFN_COL_BLOCK

    @pl.when(pl.program_id(1) == 0)
    def _():
        hist_ref[...] = jnp.zeros_like(hist_ref)

    x = x_ref[0]
    mod = mod_ref[0]
    h = (_rms(x, preg_ref[...]) * (1.0 + mod[4:5]) + mod[3:4]).astype(BF16)

    def conv_block(c0):
        cols = slice(c0, c0 + fc)
        u = _dot(h, wup_ref[:, cols])
        ext = jnp.concatenate([hist_ref[:, cols], u], axis=0)
        hist_ref[:, cols] = u[tm - SUBLANES:, :]
        out = dww_ref[2:3, cols] * u + dwb_ref[:, cols]
        for lag in (1, 2):
            shifted = pltpu.roll(ext, shift=lag, axis=0)[SUBLANES:, :]
            out = out + dww_ref[2 - lag:3 - lag, cols] * shifted
        return out

    for j in range(D_FF // fc):
        a = conv_block(j * fc)
        g = conv_block(D_FF + j * fc)
        act_s[:, j * fc:(j + 1) * fc] = (_silu(a) * g).astype(BF16)
    y = _dot(act_s[...], wdown_ref[...])
    o_ref[0] = x + mod[5:6] * _rms(y, postg_ref[...])


def _ffn_sublayer(x, mod, pre_g, post_g, w_up, dw_w, dw_b, w_down):
    bsz, t, d = x.shape
    tm = TOKENS_PER_STEP
    const = _resident
    tile = pl.BlockSpec((1, tm, d), lambda bi, ti: (bi, ti, 0))
    return pl.pallas_call(
        _ffn_kernel,
        grid=(bsz, t // tm),
        in_specs=[
            tile,
            pl.BlockSpec((1, N_MOD, d), lambda bi, ti: (bi, 0, 0)),
            const(1, d), const(1, d),
            const(d, 2 * D_FF),
            const(FFN_CONV_WIDTH, 2 * D_FF), const(1, 2 * D_FF),
            const(D_FF, d),
        ],
        out_specs=tile,
        out_shape=jax.ShapeDtypeStruct(x.shape, F32),
        scratch_shapes=[
            pltpu.VMEM((SUBLANES, 2 * D_FF), F32),
            pltpu.VMEM((tm, D_FF), BF16),
        ],
        compiler_params=pltpu.CompilerParams(
            dimension_semantics=("arbitrary", "arbitrary"), vmem_limit_bytes=VMEM_LIMIT_BYTES),
        name="convffn_sublayer",
    )(x, mod, pre_g.reshape(1, d), post_g.reshape(1, d), w_up.astype(BF16), dw_w,
      dw_b.reshape(1, 2 * D_FF), w_down.astype(BF16))


def kernel(x, c, ada_w, ada_b, pre_mix_g, post_mix_g, pre_ffn_g, post_ffn_g, hgrn_w_in, hgrn_lb_logits, hgrn_gnorm_g, hgrn_w_out, conv_w_in, conv_b_in, conv_dw_w, conv_dw_b, conv_ln_g, conv_ln_b, conv_w_out, conv_b_out, ffn_w_up, ffn_dw_w, ffn_dw_b, ffn_w_down):
    depth = ada_w.shape[0]
    bsz, _, d = x.shape
    mod = _modulation(c, ada_w, ada_b).reshape(depth, bsz, N_MOD, d)
    for i in range(depth):
        j = i // 2
        if i % 2 == 0:
            x = _hgrn_sublayer(x, mod[i], pre_mix_g[i], post_mix_g[i], hgrn_w_in[j], hgrn_lb_logits,
                               hgrn_gnorm_g[j], hgrn_w_out[j], layer=i)
        else:
            x = _conformer_sublayer(x, mod[i], pre_mix_g[i], post_mix_g[i], conv_w_in[j], conv_b_in[j],
                                    conv_dw_w[j], conv_dw_b[j], conv_ln_g[j], conv_ln_b[j],
                                    conv_w_out[j], conv_b_out[j])
        x = _ffn_sublayer(x, mod[i], pre_ffn_g[i], post_ffn_g[i], ffn_w_up[i], ffn_dw_w[i],
                          ffn_dw_b[i], ffn_w_down[i])
    return x
```

```python
import functools

import jax
import jax.numpy as jnp
from jax import lax
from jax.experimental import pallas as pl
from jax.experimental.pallas import tpu as pltpu

F32 = jnp.float32
BF16 = jnp.bfloat16

D_MODEL = 1024
N_MOD = 6
HEADS = 8
HEAD_DIM = 128
D_FF = 2816
CONV_WIDTH = 31
FFN_CONV_WIDTH = 3
EPS = 1e-6

SUBLANES = 8
TOKENS_PER_STEP = 512
HGRN_CHUNK = 64
CUMSUM_BLOCK = 256
MAX_CHUNK_LOG_DECAY = 160.0
CONV_HALO = 32
CONV_ROW_BLOCK = 128
CONV_COL_BLOCK = 128
FFN_COL_BLOCK = 256
MOD_COL_BLOCK = 1536
VMEM_LIMIT_BYTES = 56 * 1024 * 1024


def _sigmoid(x):
    return 0.5 * jnp.tanh(0.5 * x) + 0.5


def _silu(x):
    half = 0.5 * x
    return half * jnp.tanh(half) + half


def _rms(x, g):
    return x * lax.rsqrt(jnp.mean(x * x, axis=-1, keepdims=True) + EPS) * g


def _dot(a, b):
    return jnp.dot(a, b, preferred_element_type=F32)


def _dot_nt(a, b):
    return lax.dot_general(a, b, (((1,), (1,)), ((), ())), preferred_element_type=F32)


def _dot_tn(a, b):
    return lax.dot_general(a, b, (((0,), (0,)), ((), ())), preferred_element_type=F32)


def _layer_param(arr, idx):
    if arr.ndim == 2:
        arr = arr.reshape(arr.shape[0], 1, arr.shape[1])
    ndim = arr.ndim
    spec = pl.BlockSpec((None,) + arr.shape[1:], lambda bi, ti: (idx,) + (0,) * (ndim - 1),
                        pipeline_mode=pl.Buffered(1))
    return arr, spec


def _sublayer_call(body, name, x, mod, layer, params, scratch_shapes):
    bsz, t, d = x.shape
    tm = TOKENS_PER_STEP
    tile = pl.BlockSpec((1, tm, d), lambda bi, ti: (bi, ti, 0))
    mod_spec = pl.BlockSpec((None, None, N_MOD, d), lambda bi, ti: (layer, bi, 0, 0))
    operands, specs = zip(*params)
    return pl.pallas_call(
        body,
        grid=(bsz, t // tm),
        in_specs=[tile, mod_spec, *specs],
        out_specs=tile,
        out_shape=jax.ShapeDtypeStruct(x.shape, F32),
        scratch_shapes=scratch_shapes,
        compiler_params=pltpu.CompilerParams(
            dimension_semantics=("arbitrary", "arbitrary"), vmem_limit_bytes=VMEM_LIMIT_BYTES),
        name=name,
    )(x, mod, *operands)


def _mod_kernel(ct_ref, w_ref, b_ref, o_ref):
    cond = _silu(ct_ref[...])
    w = w_ref[0]
    for bi in range(cond.shape[1]):
        col = cond[:, bi:bi + 1]
        o_ref[0, bi:bi + 1, :] = jnp.sum(w * col, axis=0, keepdims=True) + b_ref[0]


def _modulation(c, ada_w, ada_b):
    depth, d, n = ada_w.shape
    bsz = c.shape[0]
    return pl.pallas_call(
        _mod_kernel,
        grid=(depth, n // MOD_COL_BLOCK),
        in_specs=[
            pl.BlockSpec((d, bsz), lambda i, j: (0, 0)),
            pl.BlockSpec((1, d, MOD_COL_BLOCK), lambda i, j: (i, 0, j)),
            pl.BlockSpec((1, 1, MOD_COL_BLOCK), lambda i, j: (i, 0, j)),
        ],
        out_specs=pl.BlockSpec((1, bsz, MOD_COL_BLOCK), lambda i, j: (i, 0, j)),
        out_shape=jax.ShapeDtypeStruct((depth, bsz, n), F32),
        compiler_params=pltpu.CompilerParams(
            dimension_semantics=("arbitrary", "arbitrary"), vmem_limit_bytes=VMEM_LIMIT_BYTES),
        name="adaln_modulation",
    )(c.T, ada_w, ada_b.reshape(depth, 1, n))


def _hgrn_kernel(x_ref, mod_ref, preg_ref, postg_ref, win_ref, lbl_ref, gn_ref, wout_ref, o_ref,
                 state_ref, q_s, k_s, b_s, v32_s, gate_s, vb_s, oh_s, *, layer):
    tm, d = x_ref.shape[1], x_ref.shape[2]
    c_len = HGRN_CHUNK
    n_chunks = tm // c_len

    @pl.when(pl.program_id(1) == 0)
    def _():
        state_ref[...] = jnp.zeros_like(state_ref)

    x = x_ref[0]
    mod = mod_ref[...]
    shift, scale = mod[0:1], mod[1:2]
    h = _rms(x, preg_ref[...]) * (1.0 + scale) + shift
    proj = _dot(h.astype(BF16), win_ref[...])

    logits = lbl_ref[...]
    ex = jnp.exp(logits - jnp.max(logits, axis=0, keepdims=True))
    lb = jnp.sum(ex[:layer + 1], axis=0, keepdims=True) / jnp.sum(ex, axis=0, keepdims=True)

    q = _silu(proj[:, :d])
    f = lb + (1.0 - lb) * _sigmoid(proj[:, d:2 * d])
    k = 1.0 - f
    logf = jnp.log(f)
    v = proj[:, 2 * d:3 * d]
    gate_s[...] = _silu(proj[:, 3 * d:])

    blk = CUMSUM_BLOCK
    row = lax.broadcasted_iota(jnp.int32, (blk, blk), 0)
    col = lax.broadcasted_iota(jnp.int32, (blk, blk), 1)
    tri = jnp.where((col <= row) & (row // c_len == col // c_len), 1.0, 0.0).astype(BF16)
    hi = logf.astype(BF16)
    rest = logf - hi.astype(F32)
    mid = rest.astype(BF16)
    lo = (rest - mid.astype(F32)).astype(BF16)
    max_decay = None
    for r in range(tm // blk):
        rows = slice(r * blk, (r + 1) * blk)
        b = _dot(tri, hi[rows]) + _dot(tri, mid[rows]) + _dot(tri, lo[rows])
        b_s[rows, :] = b
        block_max = jnp.max(-b)
        max_decay = block_max if max_decay is None else jnp.maximum(max_decay, block_max)

    q_s[...] = q
    k_s[...] = k
    v32_s[...] = v
    vb_s[...] = v.astype(BF16)
    bounded = max_decay <= MAX_CHUNK_LOG_DECAY

    causal = (lax.broadcasted_iota(jnp.int32, (c_len, c_len), 1)
              <= lax.broadcasted_iota(jnp.int32, (c_len, c_len), 0))
    row_idx = lax.broadcasted_iota(jnp.int32, (c_len, HEAD_DIM), 0)
    sub_idx = lax.broadcasted_iota(jnp.int32, (SUBLANES, HEAD_DIM), 0)

    @pl.when(bounded)
    def _factorised():
        for hh in range(HEADS):
            cs = slice(hh * HEAD_DIM, (hh + 1) * HEAD_DIM)
            st = state_ref[hh]
            for c in range(n_chunks):
                rows = slice(c * c_len, (c + 1) * c_len)
                qh, kh, bh, vb = q_s[rows, cs], k_s[rows, cs], b_s[rows, cs], vb_s[rows, cs]
                b_end = bh[c_len - 1:c_len, :]
                anchor = 0.5 * b_end
                e_anchor = jnp.exp(anchor)
                qt = (qh * jnp.exp(bh - anchor)).astype(BF16)
                kt = (kh * jnp.exp(anchor - bh)).astype(BF16)
                scores = jnp.where(causal, _dot_nt(qt, kt), 0.0).astype(BF16)
                oh_s[rows, cs] = _dot(scores, vb) + _dot_nt(qt, (st * e_anchor).astype(BF16))
                st = st * jnp.exp(b_end) + _dot_tn(vb, kt) * e_anchor
            state_ref[hh] = st

    @pl.when(jnp.logical_not(bounded))
    def _pairwise():
        def chunk_body(c, carry):
            r0 = pl.multiple_of(c * c_len, c_len)
            rows = pl.ds(r0, c_len)
            for hh in range(HEADS):
                cs = slice(hh * HEAD_DIM, (hh + 1) * HEAD_DIM)
                qh, kh, bh = q_s[rows, cs], k_s[rows, cs], b_s[rows, cs]
                qt, vb = (qh * jnp.exp(bh)).astype(BF16), vb_s[rows, cs]
                b_end = bh[c_len - 1:c_len, :]
                st = state_ref[hh]

                def s_body(s, o_acc):
                    grp = pl.ds(pl.multiple_of(r0 + (s // SUBLANES) * SUBLANES, SUBLANES), SUBLANES)
                    pick = sub_idx == s % SUBLANES
                    k_row = jnp.sum(jnp.where(pick, k_s[grp, cs], 0.0), axis=0, keepdims=True)
                    b_row = jnp.sum(jnp.where(pick, b_s[grp, cs], 0.0), axis=0, keepdims=True)
                    v_row = jnp.sum(jnp.where(pick, v32_s[grp, cs], 0.0), axis=0, keepdims=True)
                    decay = jnp.exp(jnp.minimum(bh - b_row, 0.0))
                    p = jnp.where(row_idx >= s, qh * decay * k_row, 0.0)
                    return o_acc + jnp.sum(p, axis=-1, keepdims=True) * v_row

                oh_s[rows, cs] = lax.fori_loop(0, c_len, s_body, _dot_nt(qt, st.astype(BF16)))
                k_hat = (kh * jnp.exp(b_end - bh)).astype(BF16)
                state_ref[hh] = st * jnp.exp(b_end) + _dot_tn(vb, k_hat)
            return carry

        lax.fori_loop(0, n_chunks, chunk_body, 0)

    gn = gn_ref[...]
    heads_out = []
    for hh in range(HEADS):
        cs = slice(hh * HEAD_DIM, (hh + 1) * HEAD_DIM)
        heads_out.append(_rms(oh_s[:, cs], gn))
    o = jnp.concatenate(heads_out, axis=-1) * gate_s[...]
    y = _dot(o.astype(BF16), wout_ref[...])
    o_ref[0] = x_ref[0] + mod_ref[2:3, :] * _rms(y, postg_ref[...])


def _hgrn_sublayer(x, mod, layer, pre_g, post_g, w_in, lb_logits, gnorm_g, w_out):
    d = x.shape[2]
    tm = TOKENS_PER_STEP
    j = layer // 2
    whole_logits = pl.BlockSpec(lb_logits.shape, lambda bi, ti: (0, 0), pipeline_mode=pl.Buffered(1))
    params = [_layer_param(pre_g, layer), _layer_param(post_g, layer), _layer_param(w_in.astype(BF16), j),
              (lb_logits, whole_logits), _layer_param(gnorm_g, j), _layer_param(w_out.astype(BF16), j)]
    scratch = [
        pltpu.VMEM((HEADS, HEAD_DIM, HEAD_DIM), F32),
        pltpu.VMEM((tm, d), F32),
        pltpu.VMEM((tm, d), F32),
        pltpu.VMEM((tm, d), F32),
        pltpu.VMEM((tm, d), F32),
        pltpu.VMEM((tm, d), F32),
        pltpu.VMEM((tm, d), BF16),
        pltpu.VMEM((tm, d), F32),
    ]
    return _sublayer_call(functools.partial(_hgrn_kernel, layer=layer), f"hgrn_sublayer_{layer}",
                          x, mod, layer, params, scratch)


def _conformer_kernel(x_ref, mod_ref, preg_ref, postg_ref, win_ref, bin_ref, dww_ref, dwb_ref,
                      lng_ref, lnb_ref, wout_ref, bout_ref, o_ref, ext_ref, conv_s):
    tm, d = x_ref.shape[1], x_ref.shape[2]
    halo = CONV_HALO

    @pl.when(pl.program_id(1) == 0)
    def _():
        ext_ref[0:halo, :] = jnp.zeros((halo, d), F32)

    x = x_ref[0]
    mod = mod_ref[...]
    h = _rms(x, preg_ref[...]) * (1.0 + mod[1:2]) + mod[0:1]
    u = _dot(h.astype(BF16), win_ref[...]) + bin_ref[...]
    ext_ref[halo:halo + tm, :] = u[:, :d] * _sigmoid(u[:, d:])

    rb, cw = CONV_ROW_BLOCK, CONV_COL_BLOCK
    first = halo - (CONV_WIDTH - 1)

    def row_block(i, carry):
        r0 = pl.multiple_of(i * rb, rb)
        for cb in range(d // cw):
            cols = slice(cb * cw, (cb + 1) * cw)
            y = jnp.broadcast_to(dwb_ref[:, cols], (rb, cw))
            for rr in range(SUBLANES):
                n = rb if rr == 0 else rb + SUBLANES
                z = None
                for m in range(halo // SUBLANES + 1):
                    kk = SUBLANES * m + rr - first
                    if 0 <= kk < CONV_WIDTH:
                        start = pl.multiple_of(r0 + SUBLANES * m, SUBLANES)
                        term = dww_ref[kk:kk + 1, cols] * ext_ref[pl.ds(start, n), cols]
                        z = term if z is None else z + term
                y = y + (z if rr == 0 else z[rr:rr + rb, :])
            conv_s[pl.ds(r0, rb), cols] = y
        return carry

    lax.fori_loop(0, tm // rb, row_block, 0)
    ext_ref[0:halo, :] = ext_ref[tm:tm + halo, :]

    c = conv_s[...]
    mu = jnp.mean(c, axis=-1, keepdims=True)
    cc = c - mu
    var = jnp.mean(cc * cc, axis=-1, keepdims=True)
    ln = cc * lax.rsqrt(var + EPS) * lng_ref[...] + lnb_ref[...]
    y = _dot(_silu(ln).astype(BF16), wout_ref[...]) + bout_ref[...]
    o_ref[0] = x + mod[2:3] * _rms(y, postg_ref[...])


def _conformer_sublayer(x, mod, layer, pre_g, post_g, w_in, b_in, dw_w, dw_b, ln_g, ln_b, w_out, b_out):
    d = x.shape[2]
    tm = TOKENS_PER_STEP
    j = layer // 2
    params = [_layer_param(pre_g, layer), _layer_param(post_g, layer), _layer_param(w_in.astype(BF16), j),
              _layer_param(b_in, j), _layer_param(dw_w, j), _layer_param(dw_b, j), _layer_param(ln_g, j),
              _layer_param(ln_b, j), _layer_param(w_out.astype(BF16), j), _layer_param(b_out, j)]
    scratch = [
        pltpu.VMEM((CONV_HALO + tm, d), F32),
        pltpu.VMEM((tm, d), F32),
    ]
    return _sublayer_call(_conformer_kernel, "conformer_sublayer", x, mod, layer, params, scratch)


def _ffn_kernel(x_ref, mod_ref, preg_ref, postg_ref, wup_ref, dww_ref, dwb_ref, wdown_ref, o_ref,
                hist_ref, act_s):
    tm = x_ref.shape[1]
    fc = FFN_COL_BLOCK

    @pl.when(pl.program_id(1) == 0)
    def _():
        hist_ref[...] = jnp.zeros_like(hist_ref)

    x = x_ref[0]
    mod = mod_ref[...]
    h = (_rms(x, preg_ref[...]) * (1.0 + mod[4:5]) + mod[3:4]).astype(BF16)

    def conv_block(c0):
        cols = slice(c0, c0 + fc)
        u = _dot(h, wup_ref[:, cols])
        ext = jnp.concatenate([hist_ref[:, cols], u], axis=0)
        hist_ref[:, cols] = u[tm - SUBLANES:, :]
        out = dww_ref[2:3, cols] * u + dwb_ref[:, cols]
        for lag in (1, 2):
            shifted = pltpu.roll(ext, shift=lag, axis=0)[SUBLANES:, :]
            out = out + dww_ref[2 - lag:3 - lag, cols] * shifted
        return out

    for j in range(D_FF // fc):
        a = conv_block(j * fc)
        g = conv_block(D_FF + j * fc)
        act_s[:, j * fc:(j + 1) * fc] = (_silu(a) * g).astype(BF16)
    y = _dot(act_s[...], wdown_ref[...])
    o_ref[0] = x + mod[5:6] * _rms(y, postg_ref[...])


def _ffn_sublayer(x, mod, layer, pre_g, post_g, w_up, dw_w, dw_b, w_down):
    tm = TOKENS_PER_STEP
    params = [_layer_param(pre_g, layer), _layer_param(post_g, layer), _layer_param(w_up.astype(BF16), layer),
              _layer_param(dw_w, layer), _layer_param(dw_b, layer), _layer_param(w_down.astype(BF16), layer)]
    scratch = [
        pltpu.VMEM((SUBLANES, 2 * D_FF), F32),
        pltpu.VMEM((tm, D_FF), BF16),
    ]
    return _sublayer_call(_ffn_kernel, "convffn_sublayer", x, mod, layer, params, scratch)


def kernel(x, c, ada_w, ada_b, pre_mix_g, post_mix_g, pre_ffn_g, post_ffn_g, hgrn_w_in, hgrn_lb_logits, hgrn_gnorm_g, hgrn_w_out, conv_w_in, conv_b_in, conv_dw_w, conv_dw_b, conv_ln_g, conv_ln_b, conv_w_out, conv_b_out, ffn_w_up, ffn_dw_w, ffn_dw_b, ffn_w_down):
    depth = ada_w.shape[0]
    bsz, _, d = x.shape
    mod = _modulation(c, ada_w, ada_b).reshape(depth, bsz, N_MOD, d)
    for i in range(depth):
        if i % 2 == 0:
            x = _hgrn_sublayer(x, mod, i, pre_mix_g, post_mix_g, hgrn_w_in, hgrn_lb_logits, hgrn_gnorm_g,
                               hgrn_w_out)
        else:
            x = _conformer_sublayer(x, mod, i, pre_mix_g, post_mix_g, conv_w_in, conv_b_in, conv_dw_w,
                                    conv_dw_b, conv_ln_g, conv_ln_b, conv_w_out, conv_b_out)
        x = _ffn_sublayer(x, mod, i, pre_ffn_g, post_ffn_g, ffn_w_up, ffn_dw_w, ffn_dw_b, ffn_w_down)
    return x
```

```python
import functools

import jax
import jax.numpy as jnp
from jax import lax
from jax.experimental import pallas as pl
from jax.experimental.pallas import tpu as pltpu

F32 = jnp.float32
BF16 = jnp.bfloat16

D_MODEL = 1024
N_MOD = 6
HEADS = 8
HEAD_DIM = 128
D_FF = 2816
CONV_WIDTH = 31
FFN_CONV_WIDTH = 3
EPS = 1e-6

SUBLANES = 8
TOKENS_PER_STEP = 512
HGRN_CHUNK = 64
CUMSUM_BLOCK = 256
MAX_CHUNK_LOG_DECAY = 160.0
CONV_HALO = 32
CONV_ROW_BLOCK = 128
CONV_COL_BLOCK = 128
FFN_COL_BLOCK = 256
MOD_COL_BLOCK = 2048
WEIGHT_STAGE_BYTES = 2 * 1024 * 1024
VMEM_LIMIT_BYTES = 56 * 1024 * 1024


def _sigmoid(x):
    return 0.5 * jnp.tanh(0.5 * x) + 0.5


def _silu(x):
    half = 0.5 * x
    return half * jnp.tanh(half) + half


def _rms(x, g):
    return x * lax.rsqrt(jnp.mean(x * x, axis=-1, keepdims=True) + EPS) * g


def _dot(a, b):
    return jnp.dot(a, b, preferred_element_type=F32)


def _dot_nt(a, b):
    return lax.dot_general(a, b, (((1,), (1,)), ((), ())), preferred_element_type=F32)


def _dot_tn(a, b):
    return lax.dot_general(a, b, (((0,), (0,)), ((), ())), preferred_element_type=F32)


def _whole(arr):
    ndim = arr.ndim
    return arr, pl.BlockSpec(arr.shape, lambda bi, ti: (0,) * ndim, pipeline_mode=pl.Buffered(1))


def _in_hbm(arr):
    return arr, pl.BlockSpec(memory_space=pl.ANY)


def _stage_rows(rows, cols):
    limit = max(SUBLANES, WEIGHT_STAGE_BYTES // (4 * cols))
    return max(r for r in range(SUBLANES, rows + 1, SUBLANES) if rows % r == 0 and r <= limit)


def _weight_scratch(rows, cols):
    stage = _stage_rows(rows, cols)
    return [pltpu.VMEM((rows, cols), BF16), pltpu.VMEM((2, stage, cols), F32), pltpu.SemaphoreType.DMA((2,))]


def _fetch_as_bf16(src_hbm, dst, staging, sems):
    stage = staging.shape[1]
    n_chunks = src_hbm.shape[0] // stage

    def chunk_copy(i):
        return pltpu.make_async_copy(src_hbm.at[pl.ds(i * stage, stage), :], staging.at[i % 2], sems.at[i % 2])

    chunk_copy(0).start()
    for i in range(n_chunks):
        if i + 1 < n_chunks:
            chunk_copy(i + 1).start()
        chunk_copy(i).wait()
        dst[i * stage:(i + 1) * stage, :] = staging[i % 2].astype(BF16)


def _first_step():
    return (pl.program_id(0) == 0) & (pl.program_id(1) == 0)


def _sublayer_call(body, name, x, mod, layer, params, scratch_shapes):
    bsz, t, d = x.shape
    tm = TOKENS_PER_STEP
    tile = pl.BlockSpec((1, tm, d), lambda bi, ti: (bi, ti, 0))
    mod_spec = pl.BlockSpec((None, None, N_MOD, d), lambda bi, ti: (layer, bi, 0, 0))
    operands, specs = zip(*params)
    return pl.pallas_call(
        body,
        grid=(bsz, t // tm),
        in_specs=[tile, mod_spec, *specs],
        out_specs=tile,
        out_shape=jax.ShapeDtypeStruct(x.shape, F32),
        scratch_shapes=scratch_shapes,
        compiler_params=pltpu.CompilerParams(
            dimension_semantics=("arbitrary", "arbitrary"), vmem_limit_bytes=VMEM_LIMIT_BYTES),
        name=name,
    )(x, mod, *operands)


def _mod_kernel(ct_ref, w_ref, b_ref, o_ref):
    cond = _silu(ct_ref[...])
    d = o_ref.shape[3]
    rows = w_ref.shape[2] // d
    for step in range(o_ref.shape[2] // rows):
        @pl.when(pl.program_id(1) == step)
        def _():
            for r in range(rows):
                w = w_ref[0, :, r * d:(r + 1) * d]
                for bi in range(cond.shape[1]):
                    col = cond[:, bi:bi + 1]
                    out_row = step * rows + r
                    o_ref[0, bi, out_row:out_row + 1, :] = (
                        jnp.sum(w * col, axis=0, keepdims=True) + b_ref[0, :, r * d:(r + 1) * d])


def _modulation(c, ada_w, ada_b):
    depth, d, n = ada_w.shape
    bsz = c.shape[0]
    return pl.pallas_call(
        _mod_kernel,
        grid=(depth, n // MOD_COL_BLOCK),
        in_specs=[
            pl.BlockSpec((d, bsz), lambda i, j: (0, 0)),
            pl.BlockSpec((1, d, MOD_COL_BLOCK), lambda i, j: (i, 0, j)),
            pl.BlockSpec((1, 1, MOD_COL_BLOCK), lambda i, j: (i, 0, j)),
        ],
        out_specs=pl.BlockSpec((1, bsz, n // d, d), lambda i, j: (i, 0, 0, 0)),
        out_shape=jax.ShapeDtypeStruct((depth, bsz, n // d, d), F32),
        compiler_params=pltpu.CompilerParams(
            dimension_semantics=("arbitrary", "arbitrary"), vmem_limit_bytes=VMEM_LIMIT_BYTES),
        name="adaln_modulation",
    )(c.T, ada_w, ada_b.reshape(depth, 1, n))


def _hgrn_kernel(x_ref, mod_ref, preg_ref, postg_ref, win_hbm, lbl_ref, gn_ref, wout_hbm, o_ref,
                 win_ref, win_stage, win_sems, wout_ref, wout_stage, wout_sems,
                 state_ref, q_s, k_s, b_s, v32_s, gate_s, vb_s, oh_s, *, layer):
    tm, d = x_ref.shape[1], x_ref.shape[2]
    c_len = HGRN_CHUNK
    n_chunks = tm // c_len
    j = layer // 2

    @pl.when(_first_step())
    def _():
        _fetch_as_bf16(win_hbm.at[j], win_ref, win_stage, win_sems)
        _fetch_as_bf16(wout_hbm.at[j], wout_ref, wout_stage, wout_sems)

    @pl.when(pl.program_id(1) == 0)
    def _():
        state_ref[...] = jnp.zeros_like(state_ref)

    x = x_ref[0]
    mod = mod_ref[...]
    shift, scale = mod[0:1], mod[1:2]
    h = _rms(x, preg_ref[layer:layer + 1, :]) * (1.0 + scale) + shift
    proj = _dot(h.astype(BF16), win_ref[...])

    logits = lbl_ref[...]
    ex = jnp.exp(logits - jnp.max(logits, axis=0, keepdims=True))
    lb = jnp.sum(ex[:layer + 1], axis=0, keepdims=True) / jnp.sum(ex, axis=0, keepdims=True)

    q = _silu(proj[:, :d])
    f = lb + (1.0 - lb) * _sigmoid(proj[:, d:2 * d])
    k = 1.0 - f
    logf = jnp.log(f)
    v = proj[:, 2 * d:3 * d]
    gate_s[...] = _silu(proj[:, 3 * d:])

    blk = CUMSUM_BLOCK
    row = lax.broadcasted_iota(jnp.int32, (blk, blk), 0)
    col = lax.broadcasted_iota(jnp.int32, (blk, blk), 1)
    tri = jnp.where((col <= row) & (row // c_len == col // c_len), 1.0, 0.0).astype(BF16)
    hi = logf.astype(BF16)
    rest = logf - hi.astype(F32)
    mid = rest.astype(BF16)
    lo = (rest - mid.astype(F32)).astype(BF16)
    max_decay = None
    for r in range(tm // blk):
        rows = slice(r * blk, (r + 1) * blk)
        b = _dot(tri, hi[rows]) + _dot(tri, mid[rows]) + _dot(tri, lo[rows])
        b_s[rows, :] = b
        block_max = jnp.max(-b)
        max_decay = block_max if max_decay is None else jnp.maximum(max_decay, block_max)

    q_s[...] = q
    k_s[...] = k
    v32_s[...] = v
    vb_s[...] = v.astype(BF16)
    bounded = max_decay <= MAX_CHUNK_LOG_DECAY

    causal = (lax.broadcasted_iota(jnp.int32, (c_len, c_len), 1)
              <= lax.broadcasted_iota(jnp.int32, (c_len, c_len), 0))
    row_idx = lax.broadcasted_iota(jnp.int32, (c_len, HEAD_DIM), 0)
    sub_idx = lax.broadcasted_iota(jnp.int32, (SUBLANES, HEAD_DIM), 0)

    @pl.when(bounded)
    def _factorised():
        for hh in range(HEADS):
            cs = slice(hh * HEAD_DIM, (hh + 1) * HEAD_DIM)
            st = state_ref[hh]
            for c in range(n_chunks):
                rows = slice(c * c_len, (c + 1) * c_len)
                qh, kh, bh, vb = q_s[rows, cs], k_s[rows, cs], b_s[rows, cs], vb_s[rows, cs]
                b_end = bh[c_len - 1:c_len, :]
                anchor = 0.5 * b_end
                e_anchor = jnp.exp(anchor)
                qt = (qh * jnp.exp(bh - anchor)).astype(BF16)
                kt = (kh * jnp.exp(anchor - bh)).astype(BF16)
                scores = jnp.where(causal, _dot_nt(qt, kt), 0.0).astype(BF16)
                oh_s[rows, cs] = _dot(scores, vb) + _dot_nt(qt, (st * e_anchor).astype(BF16))
                st = st * jnp.exp(b_end) + _dot_tn(vb, kt) * e_anchor
            state_ref[hh] = st

    @pl.when(jnp.logical_not(bounded))
    def _pairwise():
        def chunk_body(c, carry):
            r0 = pl.multiple_of(c * c_len, c_len)
            rows = pl.ds(r0, c_len)
            for hh in range(HEADS):
                cs = slice(hh * HEAD_DIM, (hh + 1) * HEAD_DIM)
                qh, kh, bh = q_s[rows, cs], k_s[rows, cs], b_s[rows, cs]
                qt, vb = (qh * jnp.exp(bh)).astype(BF16), vb_s[rows, cs]
                b_end = bh[c_len - 1:c_len, :]
                st = state_ref[hh]

                def s_body(s, o_acc):
                    grp = pl.ds(pl.multiple_of(r0 + (s // SUBLANES) * SUBLANES, SUBLANES), SUBLANES)
                    pick = sub_idx == s % SUBLANES
                    k_row = jnp.sum(jnp.where(pick, k_s[grp, cs], 0.0), axis=0, keepdims=True)
                    b_row = jnp.sum(jnp.where(pick, b_s[grp, cs], 0.0), axis=0, keepdims=True)
                    v_row = jnp.sum(jnp.where(pick, v32_s[grp, cs], 0.0), axis=0, keepdims=True)
                    decay = jnp.exp(jnp.minimum(bh - b_row, 0.0))
                    p = jnp.where(row_idx >= s, qh * decay * k_row, 0.0)
                    return o_acc + jnp.sum(p, axis=-1, keepdims=True) * v_row

                oh_s[rows, cs] = lax.fori_loop(0, c_len, s_body, _dot_nt(qt, st.astype(BF16)))
                k_hat = (kh * jnp.exp(b_end - bh)).astype(BF16)
                state_ref[hh] = st * jnp.exp(b_end) + _dot_tn(vb, k_hat)
            return carry

        lax.fori_loop(0, n_chunks, chunk_body, 0)

    gn = gn_ref[j:j + 1, :]
    heads_out = []
    for hh in range(HEADS):
        cs = slice(hh * HEAD_DIM, (hh + 1) * HEAD_DIM)
        heads_out.append(_rms(oh_s[:, cs], gn))
    o = jnp.concatenate(heads_out, axis=-1) * gate_s[...]
    y = _dot(o.astype(BF16), wout_ref[...])
    o_ref[0] = x_ref[0] + mod_ref[2:3, :] * _rms(y, postg_ref[layer:layer + 1, :])


def _hgrn_sublayer(x, mod, layer, pre_g, post_g, w_in, lb_logits, gnorm_g, w_out):
    d = x.shape[2]
    tm = TOKENS_PER_STEP
    params = [_whole(pre_g), _whole(post_g), _in_hbm(w_in), _whole(lb_logits), _whole(gnorm_g), _in_hbm(w_out)]
    scratch = [
        *_weight_scratch(*w_in.shape[1:]),
        *_weight_scratch(*w_out.shape[1:]),
        pltpu.VMEM((HEADS, HEAD_DIM, HEAD_DIM), F32),
        pltpu.VMEM((tm, d), F32),
        pltpu.VMEM((tm, d), F32),
        pltpu.VMEM((tm, d), F32),
        pltpu.VMEM((tm, d), F32),
        pltpu.VMEM((tm, d), F32),
        pltpu.VMEM((tm, d), BF16),
        pltpu.VMEM((tm, d), F32),
    ]
    return _sublayer_call(functools.partial(_hgrn_kernel, layer=layer), f"hgrn_sublayer_{layer}",
                          x, mod, layer, params, scratch)


def _conformer_kernel(x_ref, mod_ref, preg_ref, postg_ref, win_hbm, bin_ref, dww_ref, dwb_ref,
                      lng_ref, lnb_ref, wout_hbm, bout_ref, o_ref,
                      win_ref, win_stage, win_sems, wout_ref, wout_stage, wout_sems, ext_ref, conv_s, *, layer):
    tm, d = x_ref.shape[1], x_ref.shape[2]
    halo = CONV_HALO
    j = layer // 2
    dww_ref = dww_ref.at[j]

    def row(ref, i):
        return ref[i:i + 1, :]

    @pl.when(_first_step())
    def _():
        _fetch_as_bf16(win_hbm.at[j], win_ref, win_stage, win_sems)
        _fetch_as_bf16(wout_hbm.at[j], wout_ref, wout_stage, wout_sems)

    @pl.when(pl.program_id(1) == 0)
    def _():
        ext_ref[0:halo, :] = jnp.zeros((halo, d), F32)

    x = x_ref[0]
    mod = mod_ref[...]
    h = _rms(x, row(preg_ref, layer)) * (1.0 + mod[1:2]) + mod[0:1]
    u = _dot(h.astype(BF16), win_ref[...]) + row(bin_ref, j)
    ext_ref[halo:halo + tm, :] = u[:, :d] * _sigmoid(u[:, d:])

    rb, cw = CONV_ROW_BLOCK, CONV_COL_BLOCK
    first = halo - (CONV_WIDTH - 1)

    def row_block(i, carry):
        r0 = pl.multiple_of(i * rb, rb)
        for cb in range(d // cw):
            cols = slice(cb * cw, (cb + 1) * cw)
            y = jnp.broadcast_to(dwb_ref[j:j + 1, cols], (rb, cw))
            for rr in range(SUBLANES):
                n = rb if rr == 0 else rb + SUBLANES
                z = None
                for m in range(halo // SUBLANES + 1):
                    kk = SUBLANES * m + rr - first
                    if 0 <= kk < CONV_WIDTH:
                        start = pl.multiple_of(r0 + SUBLANES * m, SUBLANES)
                        term = dww_ref[kk:kk + 1, cols] * ext_ref[pl.ds(start, n), cols]
                        z = term if z is None else z + term
                y = y + (z if rr == 0 else z[rr:rr + rb, :])
            conv_s[pl.ds(r0, rb), cols] = y
        return carry

    lax.fori_loop(0, tm // rb, row_block, 0)
    ext_ref[0:halo, :] = ext_ref[tm:tm + halo, :]

    c = conv_s[...]
    mu = jnp.mean(c, axis=-1, keepdims=True)
    cc = c - mu
    var = jnp.mean(cc * cc, axis=-1, keepdims=True)
    ln = cc * lax.rsqrt(var + EPS) * row(lng_ref, j) + row(lnb_ref, j)
    y = _dot(_silu(ln).astype(BF16), wout_ref[...]) + row(bout_ref, j)
    o_ref[0] = x + mod[2:3] * _rms(y, row(postg_ref, layer))


def _conformer_sublayer(x, mod, layer, pre_g, post_g, w_in, b_in, dw_w, dw_b, ln_g, ln_b, w_out, b_out):
    d = x.shape[2]
    tm = TOKENS_PER_STEP
    params = [_whole(pre_g), _whole(post_g), _in_hbm(w_in), _whole(b_in), _whole(dw_w), _whole(dw_b),
              _whole(ln_g), _whole(ln_b), _in_hbm(w_out), _whole(b_out)]
    scratch = [
        *_weight_scratch(*w_in.shape[1:]),
        *_weight_scratch(*w_out.shape[1:]),
        pltpu.VMEM((CONV_HALO + tm, d), F32),
        pltpu.VMEM((tm, d), F32),
    ]
    return _sublayer_call(functools.partial(_conformer_kernel, layer=layer), "conformer_sublayer",
                          x, mod, layer, params, scratch)


def _ffn_kernel(x_ref, mod_ref, preg_ref, postg_ref, wup_hbm, dww_ref, dwb_ref, wdown_hbm, o_ref,
                wup_ref, wup_stage, wup_sems, wdown_ref, wdown_stage, wdown_sems, hist_ref, act_s, *, layer):
    tm = x_ref.shape[1]
    fc = FFN_COL_BLOCK
    dww_ref = dww_ref.at[layer]

    @pl.when(_first_step())
    def _():
        _fetch_as_bf16(wup_hbm.at[layer], wup_ref, wup_stage, wup_sems)
        _fetch_as_bf16(wdown_hbm.at[layer], wdown_ref, wdown_stage, wdown_sems)

    @pl.when(pl.program_id(1) == 0)
    def _():
        hist_ref[...] = jnp.zeros_like(hist_ref)

    x = x_ref[0]
    mod = mod_ref[...]
    h = (_rms(x, preg_ref[layer:layer + 1, :]) * (1.0 + mod[4:5]) + mod[3:4]).astype(BF16)

    def conv_block(c0):
        cols = slice(c0, c0 + fc)
        u = _dot(h, wup_ref[:, cols])
        ext = jnp.concatenate([hist_ref[:, cols], u], axis=0)
        hist_ref[:, cols] = u[tm - SUBLANES:, :]
        out = dww_ref[2:3, cols] * u + dwb_ref[layer:layer + 1, cols]
        for lag in (1, 2):
            shifted = pltpu.roll(ext, shift=lag, axis=0)[SUBLANES:, :]
            out = out + dww_ref[2 - lag:3 - lag, cols] * shifted
        return out

    for j in range(D_FF // fc):
        a = conv_block(j * fc)
        g = conv_block(D_FF + j * fc)
        act_s[:, j * fc:(j + 1) * fc] = (_silu(a) * g).astype(BF16)
    y = _dot(act_s[...], wdown_ref[...])
    o_ref[0] = x + mod[5:6] * _rms(y, postg_ref[layer:layer + 1, :])


def _ffn_sublayer(x, mod, layer, pre_g, post_g, w_up, dw_w, dw_b, w_down):
    tm = TOKENS_PER_STEP
    params = [_whole(pre_g), _whole(post_g), _in_hbm(w_up), _whole(dw_w), _whole(dw_b), _in_hbm(w_down)]
    scratch = [
        *_weight_scratch(*w_up.shape[1:]),
        *_weight_scratch(*w_down.shape[1:]),
        pltpu.VMEM((SUBLANES, 2 * D_FF), F32),
        pltpu.VMEM((tm, D_FF), BF16),
    ]
    return _sublayer_call(functools.partial(_ffn_kernel, layer=layer), "convffn_sublayer",
                          x, mod, layer, params, scratch)


def kernel(x, c, ada_w, ada_b, pre_mix_g, post_mix_g, pre_ffn_g, post_ffn_g, hgrn_w_in, hgrn_lb_logits, hgrn_gnorm_g, hgrn_w_out, conv_w_in, conv_b_in, conv_dw_w, conv_dw_b, conv_ln_g, conv_ln_b, conv_w_out, conv_b_out, ffn_w_up, ffn_dw_w, ffn_dw_b, ffn_w_down):
    depth = ada_w.shape[0]
    mod = _modulation(c, ada_w, ada_b)
    for i in range(depth):
        if i % 2 == 0:
            x = _hgrn_sublayer(x, mod, i, pre_mix_g, post_mix_g, hgrn_w_in, hgrn_lb_logits, hgrn_gnorm_g,
                               hgrn_w_out)
        else:
            x = _conformer_sublayer(x, mod, i, pre_mix_g, post_mix_g, conv_w_in, conv_b_in, conv_dw_w,
                                    conv_dw_b, conv_ln_g, conv_ln_b, conv_w_out, conv_b_out)
        x = _ffn_sublayer(x, mod, i, pre_ffn_g, post_ffn_g, ffn_w_up, ffn_dw_w, ffn_dw_b, ffn_w_down)
    return x
```

```python
import functools

import jax
import jax.numpy as jnp
from jax import lax
from jax.experimental import pallas as pl
from jax.experimental.pallas import tpu as pltpu

F32 = jnp.float32
BF16 = jnp.bfloat16

D_MODEL = 1024
N_MOD = 6
HEADS = 8
HEAD_DIM = 128
D_FF = 2816
CONV_WIDTH = 31
FFN_CONV_WIDTH = 3
EPS = 1e-6

SUBLANES = 8
TOKENS_PER_STEP = 512
HGRN_CHUNK = 64
CUMSUM_BLOCK = 256
MAX_CHUNK_LOG_DECAY = 160.0
CONV_HALO = 32
CONV_ROW_BLOCK = 128
CONV_COL_BLOCK = 128
FFN_COL_BLOCK = 256
MOD_COL_BLOCK = 2048
WEIGHT_STAGE_BYTES = 2 * 1024 * 1024
VMEM_LIMIT_BYTES = 56 * 1024 * 1024


def _sigmoid(x):
    return 0.5 * jnp.tanh(0.5 * x) + 0.5


def _silu(x):
    half = 0.5 * x
    return half * jnp.tanh(half) + half


def _rms(x, g):
    return x * lax.rsqrt(jnp.mean(x * x, axis=-1, keepdims=True) + EPS) * g


def _dot(a, b):
    return jnp.dot(a, b, preferred_element_type=F32)


def _dot_nt(a, b):
    return lax.dot_general(a, b, (((1,), (1,)), ((), ())), preferred_element_type=F32)


def _dot_tn(a, b):
    return lax.dot_general(a, b, (((0,), (0,)), ((), ())), preferred_element_type=F32)


def _whole(arr):
    ndim = arr.ndim
    return arr, pl.BlockSpec(arr.shape, lambda bi, ti: (0,) * ndim, pipeline_mode=pl.Buffered(1))


def _in_hbm(arr):
    return arr, pl.BlockSpec(memory_space=pl.ANY)


def _stage_rows(rows, cols):
    limit = max(SUBLANES, WEIGHT_STAGE_BYTES // (4 * cols))
    return max(r for r in range(SUBLANES, rows + 1, SUBLANES) if rows % r == 0 and r <= limit)


def _weight_scratch(rows, cols):
    stage = _stage_rows(rows, cols)
    return [pltpu.VMEM((rows, cols), BF16), pltpu.VMEM((2, stage, cols), F32), pltpu.SemaphoreType.DMA((2,))]


def _fetch_as_bf16(src_hbm, dst, staging, sems):
    stage = staging.shape[1]
    n_chunks = src_hbm.shape[0] // stage

    def chunk_copy(i):
        return pltpu.make_async_copy(src_hbm.at[pl.ds(i * stage, stage), :], staging.at[i % 2], sems.at[i % 2])

    chunk_copy(0).start()
    for i in range(n_chunks):
        if i + 1 < n_chunks:
            chunk_copy(i + 1).start()
        chunk_copy(i).wait()
        dst[i * stage:(i + 1) * stage, :] = staging[i % 2].astype(BF16)


def _first_step():
    return (pl.program_id(0) == 0) & (pl.program_id(1) == 0)


def _sublayer_call(body, name, x, mod, layer, params, scratch_shapes):
    bsz, t, d = x.shape
    tm = TOKENS_PER_STEP
    tile = pl.BlockSpec((1, tm, d), lambda bi, ti: (bi, ti, 0))
    mod_spec = pl.BlockSpec((None, None, N_MOD, d), lambda bi, ti: (layer, bi, 0, 0))
    operands, specs = zip(*params)
    return pl.pallas_call(
        body,
        grid=(bsz, t // tm),
        in_specs=[tile, mod_spec, *specs],
        out_specs=tile,
        out_shape=jax.ShapeDtypeStruct(x.shape, F32),
        scratch_shapes=scratch_shapes,
        compiler_params=pltpu.CompilerParams(
            dimension_semantics=("arbitrary", "arbitrary"), vmem_limit_bytes=VMEM_LIMIT_BYTES),
        name=name,
    )(x, mod, *operands)


def _mod_kernel(ct_ref, w_ref, b_ref, o_ref):
    cond = _silu(ct_ref[...])
    d = o_ref.shape[3]
    rows = w_ref.shape[2] // d
    for step in range(o_ref.shape[2] // rows):
        @pl.when(pl.program_id(1) == step)
        def _():
            for r in range(rows):
                w = w_ref[0, :, r * d:(r + 1) * d]
                for bi in range(cond.shape[1]):
                    col = cond[:, bi:bi + 1]
                    out_row = step * rows + r
                    o_ref[0, bi, out_row:out_row + 1, :] = (
                        jnp.sum(w * col, axis=0, keepdims=True) + b_ref[0, :, r * d:(r + 1) * d])


def _modulation(c, ada_w, ada_b):
    depth, d, n = ada_w.shape
    bsz = c.shape[0]
    return pl.pallas_call(
        _mod_kernel,
        grid=(depth, n // MOD_COL_BLOCK),
        in_specs=[
            pl.BlockSpec((d, bsz), lambda i, j: (0, 0)),
            pl.BlockSpec((1, d, MOD_COL_BLOCK), lambda i, j: (i, 0, j)),
            pl.BlockSpec((1, 1, MOD_COL_BLOCK), lambda i, j: (i, 0, j)),
        ],
        out_specs=pl.BlockSpec((1, bsz, n // d, d), lambda i, j: (i, 0, 0, 0)),
        out_shape=jax.ShapeDtypeStruct((depth, bsz, n // d, d), F32),
        compiler_params=pltpu.CompilerParams(
            dimension_semantics=("arbitrary", "arbitrary"), vmem_limit_bytes=VMEM_LIMIT_BYTES),
        name="adaln_modulation",
    )(c.T, ada_w, ada_b.reshape(depth, 1, n))


def _hgrn_kernel(x_ref, mod_ref, preg_ref, postg_ref, win_hbm, lbl_ref, gn_ref, wout_hbm, o_ref,
                 win_ref, win_stage, win_sems, wout_ref, wout_stage, wout_sems,
                 state_ref, next_state_ref, q_s, k_s, b_s, v32_s, gate_s, vb_s, oh_s, *, layer):
    tm, d = x_ref.shape[1], x_ref.shape[2]
    c_len = HGRN_CHUNK
    n_chunks = tm // c_len
    j = layer // 2

    @pl.when(_first_step())
    def _():
        _fetch_as_bf16(win_hbm.at[j], win_ref, win_stage, win_sems)
        _fetch_as_bf16(wout_hbm.at[j], wout_ref, wout_stage, wout_sems)

    @pl.when(pl.program_id(1) == 0)
    def _():
        state_ref[...] = jnp.zeros_like(state_ref)

    x = x_ref[0]
    mod = mod_ref[...]
    shift, scale = mod[0:1], mod[1:2]
    h = _rms(x, preg_ref[layer:layer + 1, :]) * (1.0 + scale) + shift
    proj = _dot(h.astype(BF16), win_ref[...])

    logits = lbl_ref[...]
    ex = jnp.exp(logits - jnp.max(logits, axis=0, keepdims=True))
    lb = jnp.sum(ex[:layer + 1], axis=0, keepdims=True) / jnp.sum(ex, axis=0, keepdims=True)

    q = _silu(proj[:, :d])
    f = lb + (1.0 - lb) * _sigmoid(proj[:, d:2 * d])
    k = 1.0 - f
    logf = jnp.log(f)
    v = proj[:, 2 * d:3 * d]
    gate_s[...] = _silu(proj[:, 3 * d:])

    blk = CUMSUM_BLOCK
    row = lax.broadcasted_iota(jnp.int32, (blk, blk), 0)
    col = lax.broadcasted_iota(jnp.int32, (blk, blk), 1)
    tri = jnp.where((col <= row) & (row // c_len == col // c_len), 1.0, 0.0).astype(BF16)
    hi = logf.astype(BF16)
    rest = logf - hi.astype(F32)
    mid = rest.astype(BF16)
    lo = (rest - mid.astype(F32)).astype(BF16)
    max_decay = None
    for r in range(tm // blk):
        rows = slice(r * blk, (r + 1) * blk)
        b = _dot(tri, hi[rows]) + _dot(tri, mid[rows]) + _dot(tri, lo[rows])
        b_s[rows, :] = b
        block_max = jnp.max(-b)
        max_decay = block_max if max_decay is None else jnp.maximum(max_decay, block_max)

    q_s[...] = q
    k_s[...] = k
    v32_s[...] = v
    vb_s[...] = v.astype(BF16)
    bounded = max_decay <= MAX_CHUNK_LOG_DECAY

    causal = (lax.broadcasted_iota(jnp.int32, (c_len, c_len), 1)
              <= lax.broadcasted_iota(jnp.int32, (c_len, c_len), 0))
    row_idx = lax.broadcasted_iota(jnp.int32, (c_len, HEAD_DIM), 0)
    sub_idx = lax.broadcasted_iota(jnp.int32, (SUBLANES, HEAD_DIM), 0)

    def _factorised():
        for hh in range(HEADS):
            cs = slice(hh * HEAD_DIM, (hh + 1) * HEAD_DIM)
            st = state_ref[hh]
            for c in range(n_chunks):
                rows = slice(c * c_len, (c + 1) * c_len)
                qh, kh, bh, vb = q_s[rows, cs], k_s[rows, cs], b_s[rows, cs], vb_s[rows, cs]
                b_end = bh[c_len - 1:c_len, :]
                anchor = 0.5 * b_end
                e_anchor = jnp.exp(anchor)
                qt = (qh * jnp.exp(bh - anchor)).astype(BF16)
                kt = (kh * jnp.exp(anchor - bh)).astype(BF16)
                scores = jnp.where(causal, _dot_nt(qt, kt), 0.0).astype(BF16)
                oh_s[rows, cs] = _dot(scores, vb) + _dot(qt, (st * e_anchor).T.astype(BF16))
                st = st * jnp.exp(b_end) + _dot_tn(vb, kt) * e_anchor
            next_state_ref[hh] = st

    def _pairwise():
        def chunk_body(c, carry):
            r0 = pl.multiple_of(c * c_len, c_len)
            rows = pl.ds(r0, c_len)
            for hh in range(HEADS):
                cs = slice(hh * HEAD_DIM, (hh + 1) * HEAD_DIM)
                qh, kh, bh = q_s[rows, cs], k_s[rows, cs], b_s[rows, cs]
                qt, vb = (qh * jnp.exp(bh)).astype(BF16), vb_s[rows, cs]
                b_end = bh[c_len - 1:c_len, :]
                st = state_ref[hh]

                def s_body(s, o_acc):
                    grp = pl.ds(pl.multiple_of(r0 + (s // SUBLANES) * SUBLANES, SUBLANES), SUBLANES)
                    pick = sub_idx == s % SUBLANES
                    k_row = jnp.sum(jnp.where(pick, k_s[grp, cs], 0.0), axis=0, keepdims=True)
                    b_row = jnp.sum(jnp.where(pick, b_s[grp, cs], 0.0), axis=0, keepdims=True)
                    v_row = jnp.sum(jnp.where(pick, v32_s[grp, cs], 0.0), axis=0, keepdims=True)
                    decay = jnp.exp(jnp.minimum(bh - b_row, 0.0))
                    p = jnp.where(row_idx >= s, qh * decay * k_row, 0.0)
                    return o_acc + jnp.sum(p, axis=-1, keepdims=True) * v_row

                oh_s[rows, cs] = lax.fori_loop(0, c_len, s_body, _dot_nt(qt, st.astype(BF16)))
                k_hat = (kh * jnp.exp(b_end - bh)).astype(BF16)
                state_ref[hh] = st * jnp.exp(b_end) + _dot_tn(vb, k_hat)
            return carry

        lax.fori_loop(0, n_chunks, chunk_body, 0)

    def _finish():
        gn = gn_ref[j:j + 1, :]
        heads_out = []
        for hh in range(HEADS):
            cs = slice(hh * HEAD_DIM, (hh + 1) * HEAD_DIM)
            heads_out.append(_rms(oh_s[:, cs], gn))
        o = jnp.concatenate(heads_out, axis=-1) * gate_s[...]
        y = _dot(o.astype(BF16), wout_ref[...])
        o_ref[0] = x_ref[0] + mod_ref[2:3, :] * _rms(y, postg_ref[layer:layer + 1, :])

    _factorised()
    _finish()

    @pl.when(bounded)
    def _commit():
        state_ref[...] = next_state_ref[...]

    @pl.when(jnp.logical_not(bounded))
    def _redo():
        _pairwise()
        _finish()


def _hgrn_sublayer(x, mod, layer, pre_g, post_g, w_in, lb_logits, gnorm_g, w_out):
    d = x.shape[2]
    tm = TOKENS_PER_STEP
    params = [_whole(pre_g), _whole(post_g), _in_hbm(w_in), _whole(lb_logits), _whole(gnorm_g), _in_hbm(w_out)]
    scratch = [
        *_weight_scratch(*w_in.shape[1:]),
        *_weight_scratch(*w_out.shape[1:]),
        pltpu.VMEM((HEADS, HEAD_DIM, HEAD_DIM), F32),
        pltpu.VMEM((HEADS, HEAD_DIM, HEAD_DIM), F32),
        pltpu.VMEM((tm, d), F32),
        pltpu.VMEM((tm, d), F32),
        pltpu.VMEM((tm, d), F32),
        pltpu.VMEM((tm, d), F32),
        pltpu.VMEM((tm, d), F32),
        pltpu.VMEM((tm, d), BF16),
        pltpu.VMEM((tm, d), F32),
    ]
    return _sublayer_call(functools.partial(_hgrn_kernel, layer=layer), f"hgrn_sublayer_{layer}",
                          x, mod, layer, params, scratch)


def _conformer_kernel(x_ref, mod_ref, preg_ref, postg_ref, win_hbm, bin_ref, dww_ref, dwb_ref,
                      lng_ref, lnb_ref, wout_hbm, bout_ref, o_ref,
                      win_ref, win_stage, win_sems, wout_ref, wout_stage, wout_sems, ext_ref, conv_s, *, layer):
    tm, d = x_ref.shape[1], x_ref.shape[2]
    halo = CONV_HALO
    j = layer // 2
    dww_ref = dww_ref.at[j]

    def row(ref, i):
        return ref[i:i + 1, :]

    @pl.when(_first_step())
    def _():
        _fetch_as_bf16(win_hbm.at[j], win_ref, win_stage, win_sems)
        _fetch_as_bf16(wout_hbm.at[j], wout_ref, wout_stage, wout_sems)

    @pl.when(pl.program_id(1) == 0)
    def _():
        ext_ref[0:halo, :] = jnp.zeros((halo, d), F32)

    x = x_ref[0]
    mod = mod_ref[...]
    h = _rms(x, row(preg_ref, layer)) * (1.0 + mod[1:2]) + mod[0:1]
    u = _dot(h.astype(BF16), win_ref[...]) + row(bin_ref, j)
    ext_ref[halo:halo + tm, :] = u[:, :d] * _sigmoid(u[:, d:])

    rb, cw = CONV_ROW_BLOCK, CONV_COL_BLOCK
    first = halo - (CONV_WIDTH - 1)

    def row_block(i, carry):
        r0 = pl.multiple_of(i * rb, rb)
        for cb in range(d // cw):
            cols = slice(cb * cw, (cb + 1) * cw)
            y = jnp.broadcast_to(dwb_ref[j:j + 1, cols], (rb, cw))
            for rr in range(SUBLANES):
                n = rb if rr == 0 else rb + SUBLANES
                z = None
                for m in range(halo // SUBLANES + 1):
                    kk = SUBLANES * m + rr - first
                    if 0 <= kk < CONV_WIDTH:
                        start = pl.multiple_of(r0 + SUBLANES * m, SUBLANES)
                        term = dww_ref[kk:kk + 1, cols] * ext_ref[pl.ds(start, n), cols]
                        z = term if z is None else z + term
                y = y + (z if rr == 0 else z[rr:rr + rb, :])
            conv_s[pl.ds(r0, rb), cols] = y
        return carry

    lax.fori_loop(0, tm // rb, row_block, 0)
    ext_ref[0:halo, :] = ext_ref[tm:tm + halo, :]

    c = conv_s[...]
    mu = jnp.mean(c, axis=-1, keepdims=True)
    cc = c - mu
    var = jnp.mean(cc * cc, axis=-1, keepdims=True)
    ln = cc * lax.rsqrt(var + EPS) * row(lng_ref, j) + row(lnb_ref, j)
    y = _dot(_silu(ln).astype(BF16), wout_ref[...]) + row(bout_ref, j)
    o_ref[0] = x + mod[2:3] * _rms(y, row(postg_ref, layer))


def _conformer_sublayer(x, mod, layer, pre_g, post_g, w_in, b_in, dw_w, dw_b, ln_g, ln_b, w_out, b_out):
    d = x.shape[2]
    tm = TOKENS_PER_STEP
    params = [_whole(pre_g), _whole(post_g), _in_hbm(w_in), _whole(b_in), _whole(dw_w), _whole(dw_b),
              _whole(ln_g), _whole(ln_b), _in_hbm(w_out), _whole(b_out)]
    scratch = [
        *_weight_scratch(*w_in.shape[1:]),
        *_weight_scratch(*w_out.shape[1:]),
        pltpu.VMEM((CONV_HALO + tm, d), F32),
        pltpu.VMEM((tm, d), F32),
    ]
    return _sublayer_call(functools.partial(_conformer_kernel, layer=layer), "conformer_sublayer",
                          x, mod, layer, params, scratch)


def _ffn_kernel(x_ref, mod_ref, preg_ref, postg_ref, wup_hbm, dww_ref, dwb_ref, wdown_hbm, o_ref,
                wup_ref, wup_stage, wup_sems, wdown_ref, wdown_stage, wdown_sems, hist_ref, act_s, *, layer):
    tm = x_ref.shape[1]
    fc = FFN_COL_BLOCK
    dww_ref = dww_ref.at[layer]

    @pl.when(_first_step())
    def _():
        _fetch_as_bf16(wup_hbm.at[layer], wup_ref, wup_stage, wup_sems)
        _fetch_as_bf16(wdown_hbm.at[layer], wdown_ref, wdown_stage, wdown_sems)

    @pl.when(pl.program_id(1) == 0)
    def _():
        hist_ref[...] = jnp.zeros_like(hist_ref)

    x = x_ref[0]
    mod = mod_ref[...]
    h = (_rms(x, preg_ref[layer:layer + 1, :]) * (1.0 + mod[4:5]) + mod[3:4]).astype(BF16)

    def conv_block(c0):
        cols = slice(c0, c0 + fc)
        u = _dot(h, wup_ref[:, cols])
        ext = jnp.concatenate([hist_ref[:, cols], u], axis=0)
        hist_ref[:, cols] = u[tm - SUBLANES:, :]
        out = dww_ref[2:3, cols] * u + dwb_ref[layer:layer + 1, cols]
        for lag in (1, 2):
            shifted = pltpu.roll(ext, shift=lag, axis=0)[SUBLANES:, :]
            out = out + dww_ref[2 - lag:3 - lag, cols] * shifted
        return out

    for j in range(D_FF // fc):
        a = conv_block(j * fc)
        g = conv_block(D_FF + j * fc)
        act_s[:, j * fc:(j + 1) * fc] = (_silu(a) * g).astype(BF16)
    y = _dot(act_s[...], wdown_ref[...])
    o_ref[0] = x + mod[5:6] * _rms(y, postg_ref[layer:layer + 1, :])


def _ffn_sublayer(x, mod, layer, pre_g, post_g, w_up, dw_w, dw_b, w_down):
    tm = TOKENS_PER_STEP
    params = [_whole(pre_g), _whole(post_g), _in_hbm(w_up), _whole(dw_w), _whole(dw_b), _in_hbm(w_down)]
    scratch = [
        *_weight_scratch(*w_up.shape[1:]),
        *_weight_scratch(*w_down.shape[1:]),
        pltpu.VMEM((SUBLANES, 2 * D_FF), F32),
        pltpu.VMEM((tm, D_FF), BF16),
    ]
    return _sublayer_call(functools.partial(_ffn_kernel, layer=layer), "convffn_sublayer",
                          x, mod, layer, params, scratch)


def kernel(x, c, ada_w, ada_b, pre_mix_g, post_mix_g, pre_ffn_g, post_ffn_g, hgrn_w_in, hgrn_lb_logits, hgrn_gnorm_g, hgrn_w_out, conv_w_in, conv_b_in, conv_dw_w, conv_dw_b, conv_ln_g, conv_ln_b, conv_w_out, conv_b_out, ffn_w_up, ffn_dw_w, ffn_dw_b, ffn_w_down):
    depth = ada_w.shape[0]
    mod = _modulation(c, ada_w, ada_b)
    for i in range(depth):
        if i % 2 == 0:
            x = _hgrn_sublayer(x, mod, i, pre_mix_g, post_mix_g, hgrn_w_in, hgrn_lb_logits, hgrn_gnorm_g,
                               hgrn_w_out)
        else:
            x = _conformer_sublayer(x, mod, i, pre_mix_g, post_mix_g, conv_w_in, conv_b_in, conv_dw_w,
                                    conv_dw_b, conv_ln_g, conv_ln_b, conv_w_out, conv_b_out)
        x = _ffn_sublayer(x, mod, i, pre_ffn_g, post_ffn_g, ffn_w_up, ffn_dw_w, ffn_dw_b, ffn_w_down)
    return x
```

```python
import functools

import jax
import jax.numpy as jnp
from jax import lax
from jax.experimental import pallas as pl
from jax.experimental.pallas import tpu as pltpu

F32 = jnp.float32
BF16 = jnp.bfloat16

D_MODEL = 1024
N_MOD = 6
HEADS = 8
HEAD_DIM = 128
D_FF = 2816
CONV_WIDTH = 31
FFN_CONV_WIDTH = 3
EPS = 1e-6

SUBLANES = 8
HGRN_TOKENS_PER_STEP = 512
CONFORMER_TOKENS_PER_STEP = 1024
FFN_TOKENS_PER_STEP = 1024
HGRN_CHUNK = 64
CUMSUM_BLOCK = 256
MAX_CHUNK_LOG_DECAY = 160.0
CONV_HALO = 32
CONV_ROW_BLOCK = 128
CONV_COL_BLOCK = 128
FFN_COL_BLOCK = 256
MOD_COL_BLOCK = 2048
WEIGHT_STAGE_BYTES = 2 * 1024 * 1024
VMEM_LIMIT_BYTES = 56 * 1024 * 1024


def _sigmoid(x):
    return 0.5 * jnp.tanh(0.5 * x) + 0.5


def _silu(x):
    half = 0.5 * x
    return half * jnp.tanh(half) + half


def _rms(x, g):
    return x * lax.rsqrt(jnp.mean(x * x, axis=-1, keepdims=True) + EPS) * g


def _dot(a, b):
    return jnp.dot(a, b, preferred_element_type=F32)


def _dot_nt(a, b):
    return lax.dot_general(a, b, (((1,), (1,)), ((), ())), preferred_element_type=F32)


def _dot_tn(a, b):
    return lax.dot_general(a, b, (((0,), (0,)), ((), ())), preferred_element_type=F32)


def _whole(arr):
    ndim = arr.ndim
    return arr, pl.BlockSpec(arr.shape, lambda bi, ti: (0,) * ndim, pipeline_mode=pl.Buffered(1))


def _in_hbm(arr):
    return arr, pl.BlockSpec(memory_space=pl.ANY)


def _stage_rows(rows, cols):
    limit = max(SUBLANES, WEIGHT_STAGE_BYTES // (4 * cols))
    return max(r for r in range(SUBLANES, rows + 1, SUBLANES) if rows % r == 0 and r <= limit)


def _weight_scratch(rows, cols):
    stage = _stage_rows(rows, cols)
    return [pltpu.VMEM((rows, cols), BF16), pltpu.VMEM((2, stage, cols), F32), pltpu.SemaphoreType.DMA((2,))]


def _fetch_as_bf16(src_hbm, dst, staging, sems):
    stage = staging.shape[1]
    n_chunks = src_hbm.shape[0] // stage

    def chunk_copy(i):
        return pltpu.make_async_copy(src_hbm.at[pl.ds(i * stage, stage), :], staging.at[i % 2], sems.at[i % 2])

    chunk_copy(0).start()
    for i in range(n_chunks):
        if i + 1 < n_chunks:
            chunk_copy(i + 1).start()
        chunk_copy(i).wait()
        dst[i * stage:(i + 1) * stage, :] = staging[i % 2].astype(BF16)


def _first_step():
    return (pl.program_id(0) == 0) & (pl.program_id(1) == 0)


def _sublayer_call(body, name, x, mod, layer, params, scratch_shapes, tm):
    bsz, t, d = x.shape
    tile = pl.BlockSpec((1, tm, d), lambda bi, ti: (bi, ti, 0))
    mod_spec = pl.BlockSpec((None, None, N_MOD, d), lambda bi, ti: (layer, bi, 0, 0))
    operands, specs = zip(*params)
    return pl.pallas_call(
        body,
        grid=(bsz, t // tm),
        in_specs=[tile, mod_spec, *specs],
        out_specs=tile,
        out_shape=jax.ShapeDtypeStruct(x.shape, F32),
        scratch_shapes=scratch_shapes,
        compiler_params=pltpu.CompilerParams(
            dimension_semantics=("arbitrary", "arbitrary"), vmem_limit_bytes=VMEM_LIMIT_BYTES),
        name=name,
    )(x, mod, *operands)


def _mod_kernel(ct_ref, w_ref, b_ref, o_ref):
    cond = _silu(ct_ref[...])
    d = o_ref.shape[3]
    rows = w_ref.shape[2] // d
    for step in range(o_ref.shape[2] // rows):
        @pl.when(pl.program_id(1) == step)
        def _():
            for r in range(rows):
                w = w_ref[0, :, r * d:(r + 1) * d]
                for bi in range(cond.shape[1]):
                    col = cond[:, bi:bi + 1]
                    out_row = step * rows + r
                    o_ref[0, bi, out_row:out_row + 1, :] = (
                        jnp.sum(w * col, axis=0, keepdims=True) + b_ref[0, :, r * d:(r + 1) * d])


def _modulation(c, ada_w, ada_b):
    depth, d, n = ada_w.shape
    bsz = c.shape[0]
    return pl.pallas_call(
        _mod_kernel,
        grid=(depth, n // MOD_COL_BLOCK),
        in_specs=[
            pl.BlockSpec((d, bsz), lambda i, j: (0, 0)),
            pl.BlockSpec((1, d, MOD_COL_BLOCK), lambda i, j: (i, 0, j)),
            pl.BlockSpec((1, 1, MOD_COL_BLOCK), lambda i, j: (i, 0, j)),
        ],
        out_specs=pl.BlockSpec((1, bsz, n // d, d), lambda i, j: (i, 0, 0, 0)),
        out_shape=jax.ShapeDtypeStruct((depth, bsz, n // d, d), F32),
        compiler_params=pltpu.CompilerParams(
            dimension_semantics=("arbitrary", "arbitrary"), vmem_limit_bytes=VMEM_LIMIT_BYTES),
        name="adaln_modulation",
    )(c.T, ada_w, ada_b.reshape(depth, 1, n))


def _hgrn_kernel(x_ref, mod_ref, preg_ref, postg_ref, win_hbm, lbl_ref, gn_ref, wout_hbm, o_ref,
                 win_ref, win_stage, win_sems, wout_ref, wout_stage, wout_sems,
                 state_ref, next_state_ref, q_s, k_s, b_s, v32_s, gate_s, vb_s, oh_s, *, layer):
    tm, d = x_ref.shape[1], x_ref.shape[2]
    c_len = HGRN_CHUNK
    n_chunks = tm // c_len
    j = layer // 2

    @pl.when(_first_step())
    def _():
        _fetch_as_bf16(win_hbm.at[j], win_ref, win_stage, win_sems)
        _fetch_as_bf16(wout_hbm.at[j], wout_ref, wout_stage, wout_sems)

    @pl.when(pl.program_id(1) == 0)
    def _():
        state_ref[...] = jnp.zeros_like(state_ref)

    x = x_ref[0]
    mod = mod_ref[...]
    shift, scale = mod[0:1], mod[1:2]
    h = _rms(x, preg_ref[layer:layer + 1, :]) * (1.0 + scale) + shift
    proj = _dot(h.astype(BF16), win_ref[...])

    logits = lbl_ref[...]
    ex = jnp.exp(logits - jnp.max(logits, axis=0, keepdims=True))
    lb = jnp.sum(ex[:layer + 1], axis=0, keepdims=True) / jnp.sum(ex, axis=0, keepdims=True)

    q = _silu(proj[:, :d])
    f = lb + (1.0 - lb) * _sigmoid(proj[:, d:2 * d])
    k = 1.0 - f
    logf = jnp.log(f)
    v = proj[:, 2 * d:3 * d]
    gate_s[...] = _silu(proj[:, 3 * d:])

    blk = CUMSUM_BLOCK
    row = lax.broadcasted_iota(jnp.int32, (blk, blk), 0)
    col = lax.broadcasted_iota(jnp.int32, (blk, blk), 1)
    tri = jnp.where((col <= row) & (row // c_len == col // c_len), 1.0, 0.0).astype(BF16)
    hi = logf.astype(BF16)
    rest = logf - hi.astype(F32)
    mid = rest.astype(BF16)
    lo = (rest - mid.astype(F32)).astype(BF16)
    max_decay = None
    for r in range(tm // blk):
        rows = slice(r * blk, (r + 1) * blk)
        b = _dot(tri, hi[rows]) + _dot(tri, mid[rows]) + _dot(tri, lo[rows])
        b_s[rows, :] = b
        block_max = jnp.max(-b)
        max_decay = block_max if max_decay is None else jnp.maximum(max_decay, block_max)

    q_s[...] = q
    k_s[...] = k
    v32_s[...] = v
    vb_s[...] = v.astype(BF16)
    bounded = max_decay <= MAX_CHUNK_LOG_DECAY

    causal = (lax.broadcasted_iota(jnp.int32, (c_len, c_len), 1)
              <= lax.broadcasted_iota(jnp.int32, (c_len, c_len), 0))
    row_idx = lax.broadcasted_iota(jnp.int32, (c_len, HEAD_DIM), 0)
    sub_idx = lax.broadcasted_iota(jnp.int32, (SUBLANES, HEAD_DIM), 0)

    def _factorised():
        for hh in range(HEADS):
            cs = slice(hh * HEAD_DIM, (hh + 1) * HEAD_DIM)
            st = state_ref[hh]
            for c in range(n_chunks):
                rows = slice(c * c_len, (c + 1) * c_len)
                qh, kh, bh, vb = q_s[rows, cs], k_s[rows, cs], b_s[rows, cs], vb_s[rows, cs]
                b_end = bh[c_len - 1:c_len, :]
                anchor = 0.5 * b_end
                e_anchor = jnp.exp(anchor)
                qt = (qh * jnp.exp(bh - anchor)).astype(BF16)
                kt = (kh * jnp.exp(anchor - bh)).astype(BF16)
                scores = jnp.where(causal, _dot_nt(qt, kt), 0.0).astype(BF16)
                oh_s[rows, cs] = _dot(scores, vb) + _dot(qt, (st * e_anchor).T.astype(BF16))
                st = st * jnp.exp(b_end) + _dot_tn(vb, kt) * e_anchor
            next_state_ref[hh] = st

    def _pairwise():
        def chunk_body(c, carry):
            r0 = pl.multiple_of(c * c_len, c_len)
            rows = pl.ds(r0, c_len)
            for hh in range(HEADS):
                cs = slice(hh * HEAD_DIM, (hh + 1) * HEAD_DIM)
                qh, kh, bh = q_s[rows, cs], k_s[rows, cs], b_s[rows, cs]
                qt, vb = (qh * jnp.exp(bh)).astype(BF16), vb_s[rows, cs]
                b_end = bh[c_len - 1:c_len, :]
                st = state_ref[hh]

                def s_body(s, o_acc):
                    grp = pl.ds(pl.multiple_of(r0 + (s // SUBLANES) * SUBLANES, SUBLANES), SUBLANES)
                    pick = sub_idx == s % SUBLANES
                    k_row = jnp.sum(jnp.where(pick, k_s[grp, cs], 0.0), axis=0, keepdims=True)
                    b_row = jnp.sum(jnp.where(pick, b_s[grp, cs], 0.0), axis=0, keepdims=True)
                    v_row = jnp.sum(jnp.where(pick, v32_s[grp, cs], 0.0), axis=0, keepdims=True)
                    decay = jnp.exp(jnp.minimum(bh - b_row, 0.0))
                    p = jnp.where(row_idx >= s, qh * decay * k_row, 0.0)
                    return o_acc + jnp.sum(p, axis=-1, keepdims=True) * v_row

                oh_s[rows, cs] = lax.fori_loop(0, c_len, s_body, _dot_nt(qt, st.astype(BF16)))
                k_hat = (kh * jnp.exp(b_end - bh)).astype(BF16)
                state_ref[hh] = st * jnp.exp(b_end) + _dot_tn(vb, k_hat)
            return carry

        lax.fori_loop(0, n_chunks, chunk_body, 0)

    def _finish():
        gn = gn_ref[j:j + 1, :]
        heads_out = []
        for hh in range(HEADS):
            cs = slice(hh * HEAD_DIM, (hh + 1) * HEAD_DIM)
            heads_out.append(_rms(oh_s[:, cs], gn))
        o = jnp.concatenate(heads_out, axis=-1) * gate_s[...]
        y = _dot(o.astype(BF16), wout_ref[...])
        o_ref[0] = x_ref[0] + mod_ref[2:3, :] * _rms(y, postg_ref[layer:layer + 1, :])

    _factorised()
    _finish()

    @pl.when(bounded)
    def _commit():
        state_ref[...] = next_state_ref[...]

    @pl.when(jnp.logical_not(bounded))
    def _redo():
        _pairwise()
        _finish()


def _hgrn_sublayer(x, mod, layer, pre_g, post_g, w_in, lb_logits, gnorm_g, w_out):
    d = x.shape[2]
    tm = HGRN_TOKENS_PER_STEP
    params = [_whole(pre_g), _whole(post_g), _in_hbm(w_in), _whole(lb_logits), _whole(gnorm_g), _in_hbm(w_out)]
    scratch = [
        *_weight_scratch(*w_in.shape[1:]),
        *_weight_scratch(*w_out.shape[1:]),
        pltpu.VMEM((HEADS, HEAD_DIM, HEAD_DIM), F32),
        pltpu.VMEM((HEADS, HEAD_DIM, HEAD_DIM), F32),
        pltpu.VMEM((tm, d), F32),
        pltpu.VMEM((tm, d), F32),
        pltpu.VMEM((tm, d), F32),
        pltpu.VMEM((tm, d), F32),
        pltpu.VMEM((tm, d), F32),
        pltpu.VMEM((tm, d), BF16),
        pltpu.VMEM((tm, d), F32),
    ]
    return _sublayer_call(functools.partial(_hgrn_kernel, layer=layer), f"hgrn_sublayer_{layer}",
                          x, mod, layer, params, scratch, tm)


def _conformer_kernel(x_ref, mod_ref, preg_ref, postg_ref, win_hbm, bin_ref, dww_ref, dwb_ref,
                      lng_ref, lnb_ref, wout_hbm, bout_ref, o_ref,
                      win_ref, win_stage, win_sems, wout_ref, wout_stage, wout_sems, ext_ref, conv_s, *, layer):
    tm, d = x_ref.shape[1], x_ref.shape[2]
    halo = CONV_HALO
    j = layer // 2
    dww_ref = dww_ref.at[j]

    def row(ref, i):
        return ref[i:i + 1, :]

    @pl.when(_first_step())
    def _():
        _fetch_as_bf16(win_hbm.at[j], win_ref, win_stage, win_sems)
        _fetch_as_bf16(wout_hbm.at[j], wout_ref, wout_stage, wout_sems)

    @pl.when(pl.program_id(1) == 0)
    def _():
        ext_ref[0:halo, :] = jnp.zeros((halo, d), F32)

    x = x_ref[0]
    mod = mod_ref[...]
    h = _rms(x, row(preg_ref, layer)) * (1.0 + mod[1:2]) + mod[0:1]
    u = _dot(h.astype(BF16), win_ref[...]) + row(bin_ref, j)
    ext_ref[halo:halo + tm, :] = u[:, :d] * _sigmoid(u[:, d:])

    rb, cw = CONV_ROW_BLOCK, CONV_COL_BLOCK
    first = halo - (CONV_WIDTH - 1)

    def row_block(i, carry):
        r0 = pl.multiple_of(i * rb, rb)
        for cb in range(d // cw):
            cols = slice(cb * cw, (cb + 1) * cw)
            y = jnp.broadcast_to(dwb_ref[j:j + 1, cols], (rb, cw))
            for rr in range(SUBLANES):
                n = rb if rr == 0 else rb + SUBLANES
                z = None
                for m in range(halo // SUBLANES + 1):
                    kk = SUBLANES * m + rr - first
                    if 0 <= kk < CONV_WIDTH:
                        start = pl.multiple_of(r0 + SUBLANES * m, SUBLANES)
                        term = dww_ref[kk:kk + 1, cols] * ext_ref[pl.ds(start, n), cols]
                        z = term if z is None else z + term
                y = y + (z if rr == 0 else z[rr:rr + rb, :])
            conv_s[pl.ds(r0, rb), cols] = y
        return carry

    lax.fori_loop(0, tm // rb, row_block, 0)
    ext_ref[0:halo, :] = ext_ref[tm:tm + halo, :]

    c = conv_s[...]
    mu = jnp.mean(c, axis=-1, keepdims=True)
    cc = c - mu
    var = jnp.mean(cc * cc, axis=-1, keepdims=True)
    ln = cc * lax.rsqrt(var + EPS) * row(lng_ref, j) + row(lnb_ref, j)
    y = _dot(_silu(ln).astype(BF16), wout_ref[...]) + row(bout_ref, j)
    o_ref[0] = x + mod[2:3] * _rms(y, row(postg_ref, layer))


def _conformer_sublayer(x, mod, layer, pre_g, post_g, w_in, b_in, dw_w, dw_b, ln_g, ln_b, w_out, b_out):
    d = x.shape[2]
    tm = CONFORMER_TOKENS_PER_STEP
    params = [_whole(pre_g), _whole(post_g), _in_hbm(w_in), _whole(b_in), _whole(dw_w), _whole(dw_b),
              _whole(ln_g), _whole(ln_b), _in_hbm(w_out), _whole(b_out)]
    scratch = [
        *_weight_scratch(*w_in.shape[1:]),
        *_weight_scratch(*w_out.shape[1:]),
        pltpu.VMEM((CONV_HALO + tm, d), F32),
        pltpu.VMEM((tm, d), F32),
    ]
    return _sublayer_call(functools.partial(_conformer_kernel, layer=layer), "conformer_sublayer",
                          x, mod, layer, params, scratch, tm)


def _ffn_kernel(x_ref, mod_ref, preg_ref, postg_ref, wup_hbm, dww_ref, dwb_ref, wdown_hbm, o_ref,
                wup_ref, wup_stage, wup_sems, wdown_ref, wdown_stage, wdown_sems, hist_ref, act_s, *, layer):
    tm = x_ref.shape[1]
    fc = FFN_COL_BLOCK
    dww_ref = dww_ref.at[layer]

    @pl.when(_first_step())
    def _():
        _fetch_as_bf16(wup_hbm.at[layer], wup_ref, wup_stage, wup_sems)
        _fetch_as_bf16(wdown_hbm.at[layer], wdown_ref, wdown_stage, wdown_sems)

    @pl.when(pl.program_id(1) == 0)
    def _():
        hist_ref[...] = jnp.zeros_like(hist_ref)

    x = x_ref[0]
    mod = mod_ref[...]
    h = (_rms(x, preg_ref[layer:layer + 1, :]) * (1.0 + mod[4:5]) + mod[3:4]).astype(BF16)

    def conv_block(c0):
        cols = slice(c0, c0 + fc)
        u = _dot(h, wup_ref[:, cols])
        ext = jnp.concatenate([hist_ref[:, cols], u], axis=0)
        hist_ref[:, cols] = u[tm - SUBLANES:, :]
        out = dww_ref[2:3, cols] * u + dwb_ref[layer:layer + 1, cols]
        for lag in (1, 2):
            shifted = pltpu.roll(ext, shift=lag, axis=0)[SUBLANES:, :]
            out = out + dww_ref[2 - lag:3 - lag, cols] * shifted
        return out

    for j in range(D_FF // fc):
        a = conv_block(j * fc)
        g = conv_block(D_FF + j * fc)
        act_s[:, j * fc:(j + 1) * fc] = (_silu(a) * g).astype(BF16)
    y = _dot(act_s[...], wdown_ref[...])
    o_ref[0] = x + mod[5:6] * _rms(y, postg_ref[layer:layer + 1, :])


def _ffn_sublayer(x, mod, layer, pre_g, post_g, w_up, dw_w, dw_b, w_down):
    tm = FFN_TOKENS_PER_STEP
    params = [_whole(pre_g), _whole(post_g), _in_hbm(w_up), _whole(dw_w), _whole(dw_b), _in_hbm(w_down)]
    scratch = [
        *_weight_scratch(*w_up.shape[1:]),
        *_weight_scratch(*w_down.shape[1:]),
        pltpu.VMEM((SUBLANES, 2 * D_FF), F32),
        pltpu.VMEM((tm, D_FF), BF16),
    ]
    return _sublayer_call(functools.partial(_ffn_kernel, layer=layer), "convffn_sublayer",
                          x, mod, layer, params, scratch, tm)


def kernel(x, c, ada_w, ada_b, pre_mix_g, post_mix_g, pre_ffn_g, post_ffn_g, hgrn_w_in, hgrn_lb_logits, hgrn_gnorm_g, hgrn_w_out, conv_w_in, conv_b_in, conv_dw_w, conv_dw_b, conv_ln_g, conv_ln_b, conv_w_out, conv_b_out, ffn_w_up, ffn_dw_w, ffn_dw_b, ffn_w_down):
    depth = ada_w.shape[0]
    mod = _modulation(c, ada_w, ada_b)
    for i in range(depth):
        if i % 2 == 0:
            x = _hgrn_sublayer(x, mod, i, pre_mix_g, post_mix_g, hgrn_w_in, hgrn_lb_logits, hgrn_gnorm_g,
                               hgrn_w_out)
        else:
            x = _conformer_sublayer(x, mod, i, pre_mix_g, post_mix_g, conv_w_in, conv_b_in, conv_dw_w,
                                    conv_dw_b, conv_ln_g, conv_ln_b, conv_w_out, conv_b_out)
        x = _ffn_sublayer(x, mod, i, pre_ffn_g, post_ffn_g, ffn_w_up, ffn_dw_w, ffn_dw_b, ffn_w_down)
    return x
```

```python
import functools

import jax
import jax.numpy as jnp
from jax import lax
from jax.experimental import pallas as pl
from jax.experimental.pallas import tpu as pltpu

F32 = jnp.float32
BF16 = jnp.bfloat16

D_MODEL = 1024
N_MOD = 6
HEADS = 8
HEAD_DIM = 128
D_FF = 2816
CONV_WIDTH = 31
EPS = 1e-6

SUBLANES = 8
HGRN_TOKENS_PER_STEP = 512
CONFORMER_TOKENS_PER_STEP = 1024
FFN_TOKENS_PER_STEP = 1024
HGRN_CHUNK = 64
CUMSUM_BLOCK = 256
MAX_CHUNK_LOG_DECAY = 160.0
CONV_HALO = 32
CONV_ROW_BLOCK = 128
CONV_COL_BLOCK = 128
FFN_COL_BLOCK = 256
MOD_COL_BLOCK = 2048
WEIGHT_STAGE_BYTES = 2 * 1024 * 1024
VMEM_LIMIT_BYTES = 56 * 1024 * 1024


def _sigmoid(x):
    return 0.5 * jnp.tanh(0.5 * x) + 0.5


def _silu(x):
    half = 0.5 * x
    return half * jnp.tanh(half) + half


def _rms(x, g):
    return x * lax.rsqrt(jnp.mean(x * x, axis=-1, keepdims=True) + EPS) * g


def _dot(a, b):
    return jnp.dot(a, b, preferred_element_type=F32)


def _dot_nt(a, b):
    return lax.dot_general(a, b, (((1,), (1,)), ((), ())), preferred_element_type=F32)


def _dot_tn(a, b):
    return lax.dot_general(a, b, (((0,), (0,)), ((), ())), preferred_element_type=F32)


def _whole(arr):
    ndim = arr.ndim
    return arr, pl.BlockSpec(arr.shape, lambda bi, ti: (0,) * ndim, pipeline_mode=pl.Buffered(1))


def _in_hbm(arr):
    return arr, pl.BlockSpec(memory_space=pl.ANY)


def _stage_rows(rows, cols):
    limit = max(SUBLANES, WEIGHT_STAGE_BYTES // (4 * cols))
    return max(r for r in range(SUBLANES, rows + 1, SUBLANES) if rows % r == 0 and r <= limit)


def _weight_scratch(rows, cols):
    stage = _stage_rows(rows, cols)
    return [pltpu.VMEM((rows, cols), BF16), pltpu.VMEM((2, stage, cols), F32), pltpu.SemaphoreType.DMA((2,))]


def _fetch_as_bf16(src_hbm, dst, staging, sems):
    stage = staging.shape[1]
    n_chunks = src_hbm.shape[0] // stage

    def chunk_copy(i):
        return pltpu.make_async_copy(src_hbm.at[pl.ds(i * stage, stage), :], staging.at[i % 2], sems.at[i % 2])

    chunk_copy(0).start()
    for i in range(n_chunks):
        if i + 1 < n_chunks:
            chunk_copy(i + 1).start()
        chunk_copy(i).wait()
        dst[i * stage:(i + 1) * stage, :] = staging[i % 2].astype(BF16)


def _first_step():
    return (pl.program_id(0) == 0) & (pl.program_id(1) == 0)


def _sublayer_call(body, name, x, mod, layer, params, scratch_shapes, tm):
    bsz, t, d = x.shape
    assert t % tm == 0 and d == D_MODEL, (x.shape, tm)
    tile = pl.BlockSpec((1, tm, d), lambda bi, ti: (bi, ti, 0))
    mod_spec = pl.BlockSpec((None, None, N_MOD, d), lambda bi, ti: (layer, bi, 0, 0))
    operands, specs = zip(*params)
    return pl.pallas_call(
        body,
        grid=(bsz, t // tm),
        in_specs=[tile, mod_spec, *specs],
        out_specs=tile,
        out_shape=jax.ShapeDtypeStruct(x.shape, F32),
        scratch_shapes=scratch_shapes,
        compiler_params=pltpu.CompilerParams(
            dimension_semantics=("arbitrary", "arbitrary"), vmem_limit_bytes=VMEM_LIMIT_BYTES),
        name=name,
    )(x, mod, *operands)


def _mod_kernel(ct_ref, w_ref, b_ref, o_ref):
    cond = _silu(ct_ref[...])
    d = o_ref.shape[3]
    rows = w_ref.shape[2] // d
    for step in range(o_ref.shape[2] // rows):
        @pl.when(pl.program_id(1) == step)
        def _():
            for r in range(rows):
                w = w_ref[0, :, r * d:(r + 1) * d]
                for bi in range(cond.shape[1]):
                    col = cond[:, bi:bi + 1]
                    out_row = step * rows + r
                    o_ref[0, bi, out_row:out_row + 1, :] = (
                        jnp.sum(w * col, axis=0, keepdims=True) + b_ref[0, :, r * d:(r + 1) * d])


def _modulation(c, ada_w, ada_b):
    depth, d, n = ada_w.shape
    bsz = c.shape[0]
    assert n == N_MOD * d and n % MOD_COL_BLOCK == 0 and MOD_COL_BLOCK % d == 0, ada_w.shape
    return pl.pallas_call(
        _mod_kernel,
        grid=(depth, n // MOD_COL_BLOCK),
        in_specs=[
            pl.BlockSpec((d, bsz), lambda i, j: (0, 0)),
            pl.BlockSpec((1, d, MOD_COL_BLOCK), lambda i, j: (i, 0, j)),
            pl.BlockSpec((1, 1, MOD_COL_BLOCK), lambda i, j: (i, 0, j)),
        ],
        out_specs=pl.BlockSpec((1, bsz, n // d, d), lambda i, j: (i, 0, 0, 0)),
        out_shape=jax.ShapeDtypeStruct((depth, bsz, n // d, d), F32),
        compiler_params=pltpu.CompilerParams(
            dimension_semantics=("arbitrary", "arbitrary"), vmem_limit_bytes=VMEM_LIMIT_BYTES),
        name="adaln_modulation",
    )(c.T, ada_w, ada_b.reshape(depth, 1, n))


def _hgrn_kernel(x_ref, mod_ref, preg_ref, postg_ref, win_hbm, lbl_ref, gn_ref, wout_hbm, o_ref,
                 win_ref, win_stage, win_sems, wout_ref, wout_stage, wout_sems,
                 state_ref, next_state_ref, q_s, k_s, b_s, v32_s, gate_s, vb_s, oh_s, *, layer):
    tm, d = x_ref.shape[1], x_ref.shape[2]
    c_len = HGRN_CHUNK
    n_chunks = tm // c_len
    j = layer // 2

    @pl.when(_first_step())
    def _():
        _fetch_as_bf16(win_hbm.at[j], win_ref, win_stage, win_sems)
        _fetch_as_bf16(wout_hbm.at[j], wout_ref, wout_stage, wout_sems)

    @pl.when(pl.program_id(1) == 0)
    def _():
        state_ref[...] = jnp.zeros_like(state_ref)

    x = x_ref[0]
    mod = mod_ref[...]
    shift, scale = mod[0:1], mod[1:2]
    h = _rms(x, preg_ref[layer:layer + 1, :]) * (1.0 + scale) + shift
    proj = _dot(h.astype(BF16), win_ref[...])

    logits = lbl_ref[...]
    ex = jnp.exp(logits - jnp.max(logits, axis=0, keepdims=True))
    lb = jnp.sum(ex[:layer + 1], axis=0, keepdims=True) / jnp.sum(ex, axis=0, keepdims=True)

    q = _silu(proj[:, :d])
    f = lb + (1.0 - lb) * _sigmoid(proj[:, d:2 * d])
    k = 1.0 - f
    logf = jnp.log(f)
    v = proj[:, 2 * d:3 * d]
    gate_s[...] = _silu(proj[:, 3 * d:])

    blk = CUMSUM_BLOCK
    row = lax.broadcasted_iota(jnp.int32, (blk, blk), 0)
    col = lax.broadcasted_iota(jnp.int32, (blk, blk), 1)
    tri = jnp.where((col <= row) & (row // c_len == col // c_len), 1.0, 0.0).astype(BF16)
    hi = logf.astype(BF16)
    rest = logf - hi.astype(F32)
    mid = rest.astype(BF16)
    lo = (rest - mid.astype(F32)).astype(BF16)
    max_decay = None
    for r in range(tm // blk):
        rows = slice(r * blk, (r + 1) * blk)
        b = _dot(tri, hi[rows]) + _dot(tri, mid[rows]) + _dot(tri, lo[rows])
        b_s[rows, :] = b
        block_max = jnp.max(-b)
        max_decay = block_max if max_decay is None else jnp.maximum(max_decay, block_max)

    q_s[...] = q
    k_s[...] = k
    v32_s[...] = v
    vb_s[...] = v.astype(BF16)
    bounded = max_decay <= MAX_CHUNK_LOG_DECAY

    causal = (lax.broadcasted_iota(jnp.int32, (c_len, c_len), 1)
              <= lax.broadcasted_iota(jnp.int32, (c_len, c_len), 0))
    row_idx = lax.broadcasted_iota(jnp.int32, (c_len, HEAD_DIM), 0)
    sub_idx = lax.broadcasted_iota(jnp.int32, (SUBLANES, HEAD_DIM), 0)

    def _factorised():
        for hh in range(HEADS):
            cs = slice(hh * HEAD_DIM, (hh + 1) * HEAD_DIM)
            st = state_ref[hh]
            for c in range(n_chunks):
                rows = slice(c * c_len, (c + 1) * c_len)
                qh, kh, bh, vb = q_s[rows, cs], k_s[rows, cs], b_s[rows, cs], vb_s[rows, cs]
                b_end = bh[c_len - 1:c_len, :]
                anchor = 0.5 * b_end
                e_anchor = jnp.exp(anchor)
                qt = (qh * jnp.exp(bh - anchor)).astype(BF16)
                kt = (kh * jnp.exp(anchor - bh)).astype(BF16)
                scores = jnp.where(causal, _dot_nt(qt, kt), 0.0).astype(BF16)
                oh_s[rows, cs] = _dot(scores, vb) + _dot(qt, (st * e_anchor).T.astype(BF16))
                st = st * jnp.exp(b_end) + _dot_tn(vb, kt) * e_anchor
            next_state_ref[hh] = st

    def _pairwise():
        def chunk_body(c, carry):
            r0 = pl.multiple_of(c * c_len, c_len)
            rows = pl.ds(r0, c_len)
            for hh in range(HEADS):
                cs = slice(hh * HEAD_DIM, (hh + 1) * HEAD_DIM)
                qh, kh, bh = q_s[rows, cs], k_s[rows, cs], b_s[rows, cs]
                qt, vb = (qh * jnp.exp(bh)).astype(BF16), vb_s[rows, cs]
                b_end = bh[c_len - 1:c_len, :]
                st = state_ref[hh]

                def s_body(s, o_acc):
                    grp = pl.ds(pl.multiple_of(r0 + (s // SUBLANES) * SUBLANES, SUBLANES), SUBLANES)
                    pick = sub_idx == s % SUBLANES
                    k_row = jnp.sum(jnp.where(pick, k_s[grp, cs], 0.0), axis=0, keepdims=True)
                    b_row = jnp.sum(jnp.where(pick, b_s[grp, cs], 0.0), axis=0, keepdims=True)
                    v_row = jnp.sum(jnp.where(pick, v32_s[grp, cs], 0.0), axis=0, keepdims=True)
                    decay = jnp.exp(jnp.minimum(bh - b_row, 0.0))
                    p = jnp.where(row_idx >= s, qh * decay * k_row, 0.0)
                    return o_acc + jnp.sum(p, axis=-1, keepdims=True) * v_row

                oh_s[rows, cs] = lax.fori_loop(0, c_len, s_body, _dot_nt(qt, st.astype(BF16)))
                k_hat = (kh * jnp.exp(b_end - bh)).astype(BF16)
                state_ref[hh] = st * jnp.exp(b_end) + _dot_tn(vb, k_hat)
            return carry

        lax.fori_loop(0, n_chunks, chunk_body, 0)

    def _finish():
        gn = gn_ref[j:j + 1, :]
        heads_out = []
        for hh in range(HEADS):
            cs = slice(hh * HEAD_DIM, (hh + 1) * HEAD_DIM)
            heads_out.append(_rms(oh_s[:, cs], gn))
        o = jnp.concatenate(heads_out, axis=-1) * gate_s[...]
        y = _dot(o.astype(BF16), wout_ref[...])
        o_ref[0] = x_ref[0] + mod_ref[2:3, :] * _rms(y, postg_ref[layer:layer + 1, :])

    _factorised()
    _finish()

    @pl.when(bounded)
    def _commit():
        state_ref[...] = next_state_ref[...]

    @pl.when(jnp.logical_not(bounded))
    def _redo():
        _pairwise()
        _finish()


def _hgrn_sublayer(x, mod, layer, pre_g, post_g, w_in, lb_logits, gnorm_g, w_out):
    d = x.shape[2]
    tm = HGRN_TOKENS_PER_STEP
    assert tm % CUMSUM_BLOCK == 0 and CUMSUM_BLOCK % HGRN_CHUNK == 0
    assert w_in.shape[1:] == (d, 4 * d) and d == HEADS * HEAD_DIM, w_in.shape
    params = [_whole(pre_g), _whole(post_g), _in_hbm(w_in), _whole(lb_logits), _whole(gnorm_g), _in_hbm(w_out)]
    scratch = [
        *_weight_scratch(*w_in.shape[1:]),
        *_weight_scratch(*w_out.shape[1:]),
        pltpu.VMEM((HEADS, HEAD_DIM, HEAD_DIM), F32),
        pltpu.VMEM((HEADS, HEAD_DIM, HEAD_DIM), F32),
        pltpu.VMEM((tm, d), F32),
        pltpu.VMEM((tm, d), F32),
        pltpu.VMEM((tm, d), F32),
        pltpu.VMEM((tm, d), F32),
        pltpu.VMEM((tm, d), F32),
        pltpu.VMEM((tm, d), BF16),
        pltpu.VMEM((tm, d), F32),
    ]
    return _sublayer_call(functools.partial(_hgrn_kernel, layer=layer), f"hgrn_sublayer_{layer}",
                          x, mod, layer, params, scratch, tm)


def _conformer_kernel(x_ref, mod_ref, preg_ref, postg_ref, win_hbm, bin_ref, dww_ref, dwb_ref,
                      lng_ref, lnb_ref, wout_hbm, bout_ref, o_ref,
                      win_ref, win_stage, win_sems, wout_ref, wout_stage, wout_sems, ext_ref, conv_s, *, layer):
    tm, d = x_ref.shape[1], x_ref.shape[2]
    halo = CONV_HALO
    j = layer // 2
    dww_ref = dww_ref.at[j]

    def row(ref, i):
        return ref[i:i + 1, :]

    @pl.when(_first_step())
    def _():
        _fetch_as_bf16(win_hbm.at[j], win_ref, win_stage, win_sems)
        _fetch_as_bf16(wout_hbm.at[j], wout_ref, wout_stage, wout_sems)

    @pl.when(pl.program_id(1) == 0)
    def _():
        ext_ref[0:halo, :] = jnp.zeros((halo, d), F32)

    x = x_ref[0]
    mod = mod_ref[...]
    h = _rms(x, row(preg_ref, layer)) * (1.0 + mod[1:2]) + mod[0:1]
    u = _dot(h.astype(BF16), win_ref[...]) + row(bin_ref, j)
    ext_ref[halo:halo + tm, :] = u[:, :d] * _sigmoid(u[:, d:])

    rb, cw = CONV_ROW_BLOCK, CONV_COL_BLOCK
    first = halo - (CONV_WIDTH - 1)

    def row_block(i, carry):
        r0 = pl.multiple_of(i * rb, rb)
        for cb in range(d // cw):
            cols = slice(cb * cw, (cb + 1) * cw)
            y = jnp.broadcast_to(dwb_ref[j:j + 1, cols], (rb, cw))
            for rr in range(SUBLANES):
                n = rb if rr == 0 else rb + SUBLANES
                z = None
                for m in range(halo // SUBLANES + 1):
                    kk = SUBLANES * m + rr - first
                    if 0 <= kk < CONV_WIDTH:
                        start = pl.multiple_of(r0 + SUBLANES * m, SUBLANES)
                        term = dww_ref[kk:kk + 1, cols] * ext_ref[pl.ds(start, n), cols]
                        z = term if z is None else z + term
                y = y + (z if rr == 0 else z[rr:rr + rb, :])
            conv_s[pl.ds(r0, rb), cols] = y
        return carry

    lax.fori_loop(0, tm // rb, row_block, 0)
    ext_ref[0:halo, :] = ext_ref[tm:tm + halo, :]

    c = conv_s[...]
    mu = jnp.mean(c, axis=-1, keepdims=True)
    cc = c - mu
    var = jnp.mean(cc * cc, axis=-1, keepdims=True)
    ln = cc * lax.rsqrt(var + EPS) * row(lng_ref, j) + row(lnb_ref, j)
    y = _dot(_silu(ln).astype(BF16), wout_ref[...]) + row(bout_ref, j)
    o_ref[0] = x + mod[2:3] * _rms(y, row(postg_ref, layer))


def _conformer_sublayer(x, mod, layer, pre_g, post_g, w_in, b_in, dw_w, dw_b, ln_g, ln_b, w_out, b_out):
    d = x.shape[2]
    tm = CONFORMER_TOKENS_PER_STEP
    assert tm % CONV_ROW_BLOCK == 0 and d % CONV_COL_BLOCK == 0 and CONV_WIDTH - 1 <= CONV_HALO <= tm
    assert dw_w.shape[1:] == (CONV_WIDTH, d), dw_w.shape
    params = [_whole(pre_g), _whole(post_g), _in_hbm(w_in), _whole(b_in), _whole(dw_w), _whole(dw_b),
              _whole(ln_g), _whole(ln_b), _in_hbm(w_out), _whole(b_out)]
    scratch = [
        *_weight_scratch(*w_in.shape[1:]),
        *_weight_scratch(*w_out.shape[1:]),
        pltpu.VMEM((CONV_HALO + tm, d), F32),
        pltpu.VMEM((tm, d), F32),
    ]
    return _sublayer_call(functools.partial(_conformer_kernel, layer=layer), "conformer_sublayer",
                          x, mod, layer, params, scratch, tm)


def _ffn_kernel(x_ref, mod_ref, preg_ref, postg_ref, wup_hbm, dww_ref, dwb_ref, wdown_hbm, o_ref,
                wup_ref, wup_stage, wup_sems, wdown_ref, wdown_stage, wdown_sems, hist_ref, act_s, *, layer):
    tm = x_ref.shape[1]
    fc = FFN_COL_BLOCK
    dww_ref = dww_ref.at[layer]

    @pl.when(_first_step())
    def _():
        _fetch_as_bf16(wup_hbm.at[layer], wup_ref, wup_stage, wup_sems)
        _fetch_as_bf16(wdown_hbm.at[layer], wdown_ref, wdown_stage, wdown_sems)

    @pl.when(pl.program_id(1) == 0)
    def _():
        hist_ref[...] = jnp.zeros_like(hist_ref)

    x = x_ref[0]
    mod = mod_ref[...]
    h = (_rms(x, preg_ref[layer:layer + 1, :]) * (1.0 + mod[4:5]) + mod[3:4]).astype(BF16)

    def conv_block(c0):
        cols = slice(c0, c0 + fc)
        u = _dot(h, wup_ref[:, cols])
        ext = jnp.concatenate([hist_ref[:, cols], u], axis=0)
        hist_ref[:, cols] = u[tm - SUBLANES:, :]
        out = dww_ref[2:3, cols] * u + dwb_ref[layer:layer + 1, cols]
        for lag in (1, 2):
            shifted = pltpu.roll(ext, shift=lag, axis=0)[SUBLANES:, :]
            out = out + dww_ref[2 - lag:3 - lag, cols] * shifted
        return out

    for j in range(D_FF // fc):
        a = conv_block(j * fc)
        g = conv_block(D_FF + j * fc)
        act_s[:, j * fc:(j + 1) * fc] = (_silu(a) * g).astype(BF16)
    y = _dot(act_s[...], wdown_ref[...])
    o_ref[0] = x + mod[5:6] * _rms(y, postg_ref[layer:layer + 1, :])


def _ffn_sublayer(x, mod, layer, pre_g, post_g, w_up, dw_w, dw_b, w_down):
    tm = FFN_TOKENS_PER_STEP
    assert dw_w.shape[1:] == (3, 2 * D_FF) and D_FF % FFN_COL_BLOCK == 0, dw_w.shape
    params = [_whole(pre_g), _whole(post_g), _in_hbm(w_up), _whole(dw_w), _whole(dw_b), _in_hbm(w_down)]
    scratch = [
        *_weight_scratch(*w_up.shape[1:]),
        *_weight_scratch(*w_down.shape[1:]),
        pltpu.VMEM((SUBLANES, 2 * D_FF), F32),
        pltpu.VMEM((tm, D_FF), BF16),
    ]
    return _sublayer_call(functools.partial(_ffn_kernel, layer=layer), "convffn_sublayer",
                          x, mod, layer, params, scratch, tm)


def kernel(x, c, ada_w, ada_b, pre_mix_g, post_mix_g, pre_ffn_g, post_ffn_g, hgrn_w_in, hgrn_lb_logits, hgrn_gnorm_g, hgrn_w_out, conv_w_in, conv_b_in, conv_dw_w, conv_dw_b, conv_ln_g, conv_ln_b, conv_w_out, conv_b_out, ffn_w_up, ffn_dw_w, ffn_dw_b, ffn_w_down):
    depth = ada_w.shape[0]
    mod = _modulation(c, ada_w, ada_b)
    for i in range(depth):
        if i % 2 == 0:
            x = _hgrn_sublayer(x, mod, i, pre_mix_g, post_mix_g, hgrn_w_in, hgrn_lb_logits, hgrn_gnorm_g,
                               hgrn_w_out)
        else:
            x = _conformer_sublayer(x, mod, i, pre_mix_g, post_mix_g, conv_w_in, conv_b_in, conv_dw_w,
                                    conv_dw_b, conv_ln_g, conv_ln_b, conv_w_out, conv_b_out)
        x = _ffn_sublayer(x, mod, i, pre_ffn_g, post_ffn_g, ffn_w_up, ffn_dw_w, ffn_dw_b, ffn_w_down)
    return x
```

```python
import functools

import jax
import jax.numpy as jnp
from jax import lax
from jax.experimental import pallas as pl
from jax.experimental.pallas import tpu as pltpu

F32 = jnp.float32
BF16 = jnp.bfloat16

D_MODEL = 1024
N_MOD = 6
HEADS = 8
HEAD_DIM = 128
D_FF = 2816
CONV_WIDTH = 31
FFN_CONV_WIDTH = 3
EPS = 1e-6

SUBLANES = 8
HGRN_TOKENS_PER_STEP = 512
CONFORMER_TOKENS_PER_STEP = 1024
FFN_TOKENS_PER_STEP = 1024
HGRN_CHUNK = 64
CUMSUM_BLOCK = 256
MAX_CHUNK_LOG_DECAY = 160.0
CONV_HALO = 32
CONV_ROW_BLOCK = 128
CONV_COL_BLOCK = 128
FFN_COL_BLOCK = 256
MOD_COL_BLOCK = 2048
WEIGHT_STAGE_BYTES = 2 * 1024 * 1024
VMEM_LIMIT_BYTES = 56 * 1024 * 1024


def _sigmoid(x):
    return 0.5 * jnp.tanh(0.5 * x) + 0.5


def _silu(x):
    half = 0.5 * x
    return half * jnp.tanh(half) + half


def _rms(x, g):
    return x * lax.rsqrt(jnp.mean(x * x, axis=-1, keepdims=True) + EPS) * g


def _dot(a, b):
    return jnp.dot(a, b, preferred_element_type=F32)


def _dot_nt(a, b):
    return lax.dot_general(a, b, (((1,), (1,)), ((), ())), preferred_element_type=F32)


def _dot_tn(a, b):
    return lax.dot_general(a, b, (((0,), (0,)), ((), ())), preferred_element_type=F32)


def _whole(arr):
    ndim = arr.ndim
    return arr, pl.BlockSpec(arr.shape, lambda bi, ti: (0,) * ndim, pipeline_mode=pl.Buffered(1))


def _in_hbm(arr):
    return arr, pl.BlockSpec(memory_space=pl.ANY)


def _stage_rows(rows, cols):
    limit = max(SUBLANES, WEIGHT_STAGE_BYTES // (4 * cols))
    return max(r for r in range(SUBLANES, rows + 1, SUBLANES) if rows % r == 0 and r <= limit)


def _weight_scratch(rows, cols):
    stage = _stage_rows(rows, cols)
    return [pltpu.VMEM((rows, cols), BF16), pltpu.VMEM((2, stage, cols), F32), pltpu.SemaphoreType.DMA((2,))]


def _fetch_as_bf16(src_hbm, dst, staging, sems):
    stage = staging.shape[1]
    n_chunks = src_hbm.shape[0] // stage

    def chunk_copy(i):
        return pltpu.make_async_copy(src_hbm.at[pl.ds(i * stage, stage), :], staging.at[i % 2], sems.at[i % 2])

    chunk_copy(0).start()
    for i in range(n_chunks):
        if i + 1 < n_chunks:
            chunk_copy(i + 1).start()
        chunk_copy(i).wait()
        dst[i * stage:(i + 1) * stage, :] = staging[i % 2].astype(BF16)


def _first_step():
    return (pl.program_id(0) == 0) & (pl.program_id(1) == 0)


def _sublayer_call(body, name, x, mod, layer, params, scratch_shapes, tm):
    bsz, t, d = x.shape
    assert t % tm == 0 and d == D_MODEL, (x.shape, tm)
    tile = pl.BlockSpec((1, tm, d), lambda bi, ti: (bi, ti, 0))
    mod_spec = pl.BlockSpec((None, None, N_MOD, d), lambda bi, ti: (layer, bi, 0, 0))
    operands, specs = zip(*params)
    return pl.pallas_call(
        body,
        grid=(bsz, t // tm),
        in_specs=[tile, mod_spec, *specs],
        out_specs=tile,
        out_shape=jax.ShapeDtypeStruct(x.shape, F32),
        scratch_shapes=scratch_shapes,
        compiler_params=pltpu.CompilerParams(
            dimension_semantics=("arbitrary", "arbitrary"), vmem_limit_bytes=VMEM_LIMIT_BYTES),
        name=name,
    )(x, mod, *operands)


def _mod_kernel(ct_ref, w_ref, b_ref, o_ref):
    cond = _silu(ct_ref[...])
    d = o_ref.shape[3]
    rows = w_ref.shape[2] // d
    for step in range(o_ref.shape[2] // rows):
        @pl.when(pl.program_id(1) == step)
        def _():
            for r in range(rows):
                w = w_ref[0, :, r * d:(r + 1) * d]
                for bi in range(cond.shape[1]):
                    col = cond[:, bi:bi + 1]
                    out_row = step * rows + r
                    o_ref[0, bi, out_row:out_row + 1, :] = (
                        jnp.sum(w * col, axis=0, keepdims=True) + b_ref[0, :, r * d:(r + 1) * d])


def _modulation(c, ada_w, ada_b):
    depth, d, n = ada_w.shape
    bsz = c.shape[0]
    assert n == N_MOD * d and n % MOD_COL_BLOCK == 0 and MOD_COL_BLOCK % d == 0, ada_w.shape
    return pl.pallas_call(
        _mod_kernel,
        grid=(depth, n // MOD_COL_BLOCK),
        in_specs=[
            pl.BlockSpec((d, bsz), lambda i, j: (0, 0)),
            pl.BlockSpec((1, d, MOD_COL_BLOCK), lambda i, j: (i, 0, j)),
            pl.BlockSpec((1, 1, MOD_COL_BLOCK), lambda i, j: (i, 0, j)),
        ],
        out_specs=pl.BlockSpec((1, bsz, n // d, d), lambda i, j: (i, 0, 0, 0)),
        out_shape=jax.ShapeDtypeStruct((depth, bsz, n // d, d), F32),
        compiler_params=pltpu.CompilerParams(
            dimension_semantics=("arbitrary", "arbitrary"), vmem_limit_bytes=VMEM_LIMIT_BYTES),
        name="adaln_modulation",
    )(c.T, ada_w, ada_b.reshape(depth, 1, n))


def _hgrn_kernel(x_ref, mod_ref, preg_ref, postg_ref, win_hbm, lbl_ref, gn_ref, wout_hbm, o_ref,
                 win_ref, win_stage, win_sems, wout_ref, wout_stage, wout_sems,
                 state_ref, next_state_ref, q_s, k_s, b_s, v32_s, gate_s, vb_s, oh_s, *, layer):
    tm, d = x_ref.shape[1], x_ref.shape[2]
    c_len = HGRN_CHUNK
    n_chunks = tm // c_len
    j = layer // 2

    @pl.when(_first_step())
    def _():
        _fetch_as_bf16(win_hbm.at[j], win_ref, win_stage, win_sems)
        _fetch_as_bf16(wout_hbm.at[j], wout_ref, wout_stage, wout_sems)

    @pl.when(pl.program_id(1) == 0)
    def _():
        state_ref[...] = jnp.zeros_like(state_ref)

    x = x_ref[0]
    mod = mod_ref[...]
    shift, scale = mod[0:1], mod[1:2]
    h = _rms(x, preg_ref[layer:layer + 1, :] * (1.0 + scale)) + shift
    proj = _dot(h.astype(BF16), win_ref[...])

    logits = lbl_ref[...]
    ex = jnp.exp(logits - jnp.max(logits, axis=0, keepdims=True))
    lb = jnp.sum(ex[:layer + 1], axis=0, keepdims=True) / jnp.sum(ex, axis=0, keepdims=True)

    q = _silu(proj[:, :d])
    f = lb + (1.0 - lb) * _sigmoid(proj[:, d:2 * d])
    k = 1.0 - f
    logf = jnp.log(f)
    v = proj[:, 2 * d:3 * d]
    gate_s[...] = _silu(proj[:, 3 * d:])

    blk = CUMSUM_BLOCK
    row = lax.broadcasted_iota(jnp.int32, (blk, blk), 0)
    col = lax.broadcasted_iota(jnp.int32, (blk, blk), 1)
    tri = jnp.where((col <= row) & (row // c_len == col // c_len), 1.0, 0.0).astype(BF16)
    hi = logf.astype(BF16)
    rest = logf - hi.astype(F32)
    mid = rest.astype(BF16)
    lo = (rest - mid.astype(F32)).astype(BF16)
    max_decay = None
    for r in range(tm // blk):
        rows = slice(r * blk, (r + 1) * blk)
        b = _dot(tri, hi[rows]) + _dot(tri, mid[rows]) + _dot(tri, lo[rows])
        b_s[rows, :] = b
        block_max = jnp.max(-b)
        max_decay = block_max if max_decay is None else jnp.maximum(max_decay, block_max)

    q_s[...] = q
    k_s[...] = k
    v32_s[...] = v
    vb_s[...] = v.astype(BF16)
    bounded = max_decay <= MAX_CHUNK_LOG_DECAY

    causal = (lax.broadcasted_iota(jnp.int32, (c_len, c_len), 1)
              <= lax.broadcasted_iota(jnp.int32, (c_len, c_len), 0))
    row_idx = lax.broadcasted_iota(jnp.int32, (c_len, HEAD_DIM), 0)
    sub_idx = lax.broadcasted_iota(jnp.int32, (SUBLANES, HEAD_DIM), 0)

    def _factorised():
        for hh in range(HEADS):
            cs = slice(hh * HEAD_DIM, (hh + 1) * HEAD_DIM)
            st = state_ref[hh]
            for c in range(n_chunks):
                rows = slice(c * c_len, (c + 1) * c_len)
                qh, kh, bh, vb = q_s[rows, cs], k_s[rows, cs], b_s[rows, cs], vb_s[rows, cs]
                b_end = bh[c_len - 1:c_len, :]
                anchor = 0.5 * b_end
                e_anchor = jnp.exp(anchor)
                qt = (qh * jnp.exp(bh - anchor)).astype(BF16)
                kt = (kh * jnp.exp(anchor - bh)).astype(BF16)
                scores = jnp.where(causal, _dot_nt(qt, kt), 0.0).astype(BF16)
                oh_s[rows, cs] = _dot(scores, vb) + _dot(qt, (st * e_anchor).T.astype(BF16))
                st = st * jnp.exp(b_end) + _dot_tn(vb, kt) * e_anchor
            next_state_ref[hh] = st

    def _pairwise():
        def chunk_body(c, carry):
            r0 = pl.multiple_of(c * c_len, c_len)
            rows = pl.ds(r0, c_len)
            for hh in range(HEADS):
                cs = slice(hh * HEAD_DIM, (hh + 1) * HEAD_DIM)
                qh, kh, bh = q_s[rows, cs], k_s[rows, cs], b_s[rows, cs]
                qt, vb = (qh * jnp.exp(bh)).astype(BF16), vb_s[rows, cs]
                b_end = bh[c_len - 1:c_len, :]
                st = state_ref[hh]

                def s_body(s, o_acc):
                    grp = pl.ds(pl.multiple_of(r0 + (s // SUBLANES) * SUBLANES, SUBLANES), SUBLANES)
                    pick = sub_idx == s % SUBLANES
                    k_row = jnp.sum(jnp.where(pick, k_s[grp, cs], 0.0), axis=0, keepdims=True)
                    b_row = jnp.sum(jnp.where(pick, b_s[grp, cs], 0.0), axis=0, keepdims=True)
                    v_row = jnp.sum(jnp.where(pick, v32_s[grp, cs], 0.0), axis=0, keepdims=True)
                    decay = jnp.exp(jnp.minimum(bh - b_row, 0.0))
                    p = jnp.where(row_idx >= s, qh * decay * k_row, 0.0)
                    return o_acc + jnp.sum(p, axis=-1, keepdims=True) * v_row

                oh_s[rows, cs] = lax.fori_loop(0, c_len, s_body, _dot_nt(qt, st.astype(BF16)))
                k_hat = (kh * jnp.exp(b_end - bh)).astype(BF16)
                state_ref[hh] = st * jnp.exp(b_end) + _dot_tn(vb, k_hat)
            return carry

        lax.fori_loop(0, n_chunks, chunk_body, 0)

    def _finish():
        gn = gn_ref[j:j + 1, :]
        heads_out = []
        for hh in range(HEADS):
            cs = slice(hh * HEAD_DIM, (hh + 1) * HEAD_DIM)
            heads_out.append(_rms(oh_s[:, cs], gn))
        o = jnp.concatenate(heads_out, axis=-1) * gate_s[...]
        y = _dot(o.astype(BF16), wout_ref[...])
        o_ref[0] = x_ref[0] + _rms(y, postg_ref[layer:layer + 1, :] * mod_ref[2:3, :])

    _factorised()
    _finish()

    @pl.when(bounded)
    def _commit():
        state_ref[...] = next_state_ref[...]

    @pl.when(jnp.logical_not(bounded))
    def _redo():
        _pairwise()
        _finish()


def _hgrn_sublayer(x, mod, layer, pre_g, post_g, w_in, lb_logits, gnorm_g, w_out):
    d = x.shape[2]
    tm = HGRN_TOKENS_PER_STEP
    assert tm % CUMSUM_BLOCK == 0 and CUMSUM_BLOCK % HGRN_CHUNK == 0
    assert w_in.shape[1:] == (d, 4 * d) and d == HEADS * HEAD_DIM, w_in.shape
    params = [_whole(pre_g), _whole(post_g), _in_hbm(w_in), _whole(lb_logits), _whole(gnorm_g), _in_hbm(w_out)]
    scratch = [
        *_weight_scratch(*w_in.shape[1:]),
        *_weight_scratch(*w_out.shape[1:]),
        pltpu.VMEM((HEADS, HEAD_DIM, HEAD_DIM), F32),
        pltpu.VMEM((HEADS, HEAD_DIM, HEAD_DIM), F32),
        pltpu.VMEM((tm, d), F32),
        pltpu.VMEM((tm, d), F32),
        pltpu.VMEM((tm, d), F32),
        pltpu.VMEM((tm, d), F32),
        pltpu.VMEM((tm, d), F32),
        pltpu.VMEM((tm, d), BF16),
        pltpu.VMEM((tm, d), F32),
    ]
    return _sublayer_call(functools.partial(_hgrn_kernel, layer=layer), f"hgrn_sublayer_{layer}",
                          x, mod, layer, params, scratch, tm)


def _conformer_kernel(x_ref, mod_ref, preg_ref, postg_ref, win_hbm, bin_ref, dww_ref, dwb_ref,
                      lng_ref, lnb_ref, wout_hbm, bout_ref, o_ref,
                      win_ref, win_stage, win_sems, wout_ref, wout_stage, wout_sems, ext_ref, conv_s, *, layer):
    tm, d = x_ref.shape[1], x_ref.shape[2]
    halo = CONV_HALO
    j = layer // 2
    dww_ref = dww_ref.at[j]

    def row(ref, i):
        return ref[i:i + 1, :]

    @pl.when(_first_step())
    def _():
        _fetch_as_bf16(win_hbm.at[j], win_ref, win_stage, win_sems)
        _fetch_as_bf16(wout_hbm.at[j], wout_ref, wout_stage, wout_sems)

    @pl.when(pl.program_id(1) == 0)
    def _():
        ext_ref[0:halo, :] = jnp.zeros((halo, d), F32)

    x = x_ref[0]
    mod = mod_ref[...]
    h = _rms(x, row(preg_ref, layer) * (1.0 + mod[1:2])) + mod[0:1]
    u = _dot(h.astype(BF16), win_ref[...]) + row(bin_ref, j)
    ext_ref[halo:halo + tm, :] = u[:, :d] * _sigmoid(u[:, d:])

    rb, cw = CONV_ROW_BLOCK, CONV_COL_BLOCK
    first = halo - (CONV_WIDTH - 1)

    def row_block(i, carry):
        r0 = pl.multiple_of(i * rb, rb)
        for cb in range(d // cw):
            cols = slice(cb * cw, (cb + 1) * cw)
            y = jnp.broadcast_to(dwb_ref[j:j + 1, cols], (rb, cw))
            for rr in range(SUBLANES):
                n = rb if rr == 0 else rb + SUBLANES
                z = None
                for m in range(halo // SUBLANES + 1):
                    kk = SUBLANES * m + rr - first
                    if 0 <= kk < CONV_WIDTH:
                        start = pl.multiple_of(r0 + SUBLANES * m, SUBLANES)
                        term = dww_ref[kk:kk + 1, cols] * ext_ref[pl.ds(start, n), cols]
                        z = term if z is None else z + term
                y = y + (z if rr == 0 else z[rr:rr + rb, :])
            conv_s[pl.ds(r0, rb), cols] = y
        return carry

    lax.fori_loop(0, tm // rb, row_block, 0)
    ext_ref[0:halo, :] = ext_ref[tm:tm + halo, :]

    c = conv_s[...]
    mu = jnp.mean(c, axis=-1, keepdims=True)
    cc = c - mu
    var = jnp.mean(cc * cc, axis=-1, keepdims=True)
    half = cc * lax.rsqrt(var + EPS) * (0.5 * row(lng_ref, j)) + 0.5 * row(lnb_ref, j)
    y = _dot((half * jnp.tanh(half) + half).astype(BF16), wout_ref[...]) + row(bout_ref, j)
    o_ref[0] = x + _rms(y, row(postg_ref, layer) * mod[2:3])


def _conformer_sublayer(x, mod, layer, pre_g, post_g, w_in, b_in, dw_w, dw_b, ln_g, ln_b, w_out, b_out):
    d = x.shape[2]
    tm = CONFORMER_TOKENS_PER_STEP
    assert tm % CONV_ROW_BLOCK == 0 and d % CONV_COL_BLOCK == 0 and CONV_WIDTH - 1 <= CONV_HALO <= tm
    assert dw_w.shape[1:] == (CONV_WIDTH, d), dw_w.shape
    params = [_whole(pre_g), _whole(post_g), _in_hbm(w_in), _whole(b_in), _whole(dw_w), _whole(dw_b),
              _whole(ln_g), _whole(ln_b), _in_hbm(w_out), _whole(b_out)]
    scratch = [
        *_weight_scratch(*w_in.shape[1:]),
        *_weight_scratch(*w_out.shape[1:]),
        pltpu.VMEM((CONV_HALO + tm, d), F32),
        pltpu.VMEM((tm, d), F32),
    ]
    return _sublayer_call(functools.partial(_conformer_kernel, layer=layer), "conformer_sublayer",
                          x, mod, layer, params, scratch, tm)


def _ffn_kernel(x_ref, mod_ref, preg_ref, postg_ref, wup_hbm, dww_ref, dwb_ref, wdown_hbm, o_ref,
                wup_ref, wup_stage, wup_sems, wdown_ref, wdown_stage, wdown_sems, hist_ref, act_s, *, layer):
    tm = x_ref.shape[1]
    fc = FFN_COL_BLOCK
    dww_ref = dww_ref.at[layer]

    @pl.when(_first_step())
    def _():
        _fetch_as_bf16(wup_hbm.at[layer], wup_ref, wup_stage, wup_sems)
        _fetch_as_bf16(wdown_hbm.at[layer], wdown_ref, wdown_stage, wdown_sems)

    @pl.when(pl.program_id(1) == 0)
    def _():
        hist_ref[...] = jnp.zeros_like(hist_ref)

    x = x_ref[0]
    mod = mod_ref[...]
    h = (_rms(x, preg_ref[layer:layer + 1, :] * (1.0 + mod[4:5])) + mod[3:4]).astype(BF16)

    def conv_block(c0):
        cols = slice(c0, c0 + fc)
        u = _dot(h, wup_ref[:, cols])
        ext = jnp.concatenate([hist_ref[:, cols], u], axis=0)
        hist_ref[:, cols] = u[tm - SUBLANES:, :]
        last = FFN_CONV_WIDTH - 1
        out = dww_ref[last:last + 1, cols] * u + dwb_ref[layer:layer + 1, cols]
        for lag in range(1, FFN_CONV_WIDTH):
            shifted = pltpu.roll(ext, shift=lag, axis=0)[SUBLANES:, :]
            out = out + dww_ref[last - lag:last - lag + 1, cols] * shifted
        return out

    for j in range(D_FF // fc):
        a = conv_block(j * fc)
        g = conv_block(D_FF + j * fc)
        act_s[:, j * fc:(j + 1) * fc] = (_silu(a) * g).astype(BF16)
    y = _dot(act_s[...], wdown_ref[...])
    o_ref[0] = x + _rms(y, postg_ref[layer:layer + 1, :] * mod[5:6])


def _ffn_sublayer(x, mod, layer, pre_g, post_g, w_up, dw_w, dw_b, w_down):
    tm = FFN_TOKENS_PER_STEP
    assert dw_w.shape[1:] == (FFN_CONV_WIDTH, 2 * D_FF) and FFN_CONV_WIDTH - 1 <= SUBLANES, dw_w.shape
    assert D_FF % FFN_COL_BLOCK == 0
    params = [_whole(pre_g), _whole(post_g), _in_hbm(w_up), _whole(dw_w), _whole(dw_b), _in_hbm(w_down)]
    scratch = [
        *_weight_scratch(*w_up.shape[1:]),
        *_weight_scratch(*w_down.shape[1:]),
        pltpu.VMEM((SUBLANES, 2 * D_FF), F32),
        pltpu.VMEM((tm, D_FF), BF16),
    ]
    return _sublayer_call(functools.partial(_ffn_kernel, layer=layer), "convffn_sublayer",
                          x, mod, layer, params, scratch, tm)


def kernel(x, c, ada_w, ada_b, pre_mix_g, post_mix_g, pre_ffn_g, post_ffn_g, hgrn_w_in, hgrn_lb_logits, hgrn_gnorm_g, hgrn_w_out, conv_w_in, conv_b_in, conv_dw_w, conv_dw_b, conv_ln_g, conv_ln_b, conv_w_out, conv_b_out, ffn_w_up, ffn_dw_w, ffn_dw_b, ffn_w_down):
    depth = ada_w.shape[0]
    mod = _modulation(c, ada_w, ada_b)
    for i in range(depth):
        if i % 2 == 0:
            x = _hgrn_sublayer(x, mod, i, pre_mix_g, post_mix_g, hgrn_w_in, hgrn_lb_logits, hgrn_gnorm_g,
                               hgrn_w_out)
        else:
            x = _conformer_sublayer(x, mod, i, pre_mix_g, post_mix_g, conv_w_in, conv_b_in, conv_dw_w,
                                    conv_dw_b, conv_ln_g, conv_ln_b, conv_w_out, conv_b_out)
        x = _ffn_sublayer(x, mod, i, pre_ffn_g, post_ffn_g, ffn_w_up, ffn_dw_w, ffn_dw_b, ffn_w_down)
    return x
```

```python
import functools

import jax
import jax.numpy as jnp
from jax import lax
from jax.experimental import pallas as pl
from jax.experimental.pallas import tpu as pltpu

F32 = jnp.float32
BF16 = jnp.bfloat16

D_MODEL = 1024
N_MOD = 6
HEADS = 8
HEAD_DIM = 128
D_FF = 2816
CONV_WIDTH = 31
FFN_CONV_WIDTH = 3
EPS = 1e-6

SUBLANES = 8
HGRN_TOKENS_PER_STEP = 512
CONFORMER_TOKENS_PER_STEP = 1024
FFN_TOKENS_PER_STEP = 1024
HGRN_CHUNK = 64
CUMSUM_BLOCK = 256
MAX_CHUNK_LOG_DECAY = 160.0
CONV_HALO = 32
CONV_ROW_BLOCK = 256
CONV_COL_BLOCK = 128
FFN_COL_BLOCK = 256
MOD_COL_BLOCK = 2048
WEIGHT_STAGE_BYTES = 2 * 1024 * 1024
VMEM_LIMIT_BYTES = 56 * 1024 * 1024


def _sigmoid(x):
    return 0.5 * jnp.tanh(0.5 * x) + 0.5


def _silu(x):
    half = 0.5 * x
    return half * jnp.tanh(half) + half


def _rms(x, g):
    return x * lax.rsqrt(jnp.mean(x * x, axis=-1, keepdims=True) + EPS) * g


def _dot(a, b):
    return jnp.dot(a, b, preferred_element_type=F32)


def _dot_nt(a, b):
    return lax.dot_general(a, b, (((1,), (1,)), ((), ())), preferred_element_type=F32)


def _dot_tn(a, b):
    return lax.dot_general(a, b, (((0,), (0,)), ((), ())), preferred_element_type=F32)


def _whole(arr):
    ndim = arr.ndim
    return arr, pl.BlockSpec(arr.shape, lambda bi, ti: (0,) * ndim, pipeline_mode=pl.Buffered(1))


def _in_hbm(arr):
    return arr, pl.BlockSpec(memory_space=pl.ANY)


def _stage_rows(rows, cols):
    limit = max(SUBLANES, WEIGHT_STAGE_BYTES // (4 * cols))
    return max(r for r in range(SUBLANES, rows + 1, SUBLANES) if rows % r == 0 and r <= limit)


def _weight_scratch(rows, cols):
    stage = _stage_rows(rows, cols)
    return [pltpu.VMEM((rows, cols), BF16), pltpu.VMEM((2, stage, cols), F32), pltpu.SemaphoreType.DMA((2,))]


def _fetch_as_bf16(src_hbm, dst, staging, sems, scale=None):
    stage = staging.shape[1]
    n_chunks = src_hbm.shape[0] // stage

    def chunk_copy(i):
        return pltpu.make_async_copy(src_hbm.at[pl.ds(i * stage, stage), :], staging.at[i % 2], sems.at[i % 2])

    chunk_copy(0).start()
    for i in range(n_chunks):
        if i + 1 < n_chunks:
            chunk_copy(i + 1).start()
        chunk_copy(i).wait()
        chunk = staging[i % 2]
        dst[i * stage:(i + 1) * stage, :] = (chunk if scale is None else chunk * scale).astype(BF16)


def _first_step():
    return (pl.program_id(0) == 0) & (pl.program_id(1) == 0)


def _sublayer_call(body, name, x, mod, layer, params, scratch_shapes, tm):
    bsz, t, d = x.shape
    assert t % tm == 0 and d == D_MODEL, (x.shape, tm)
    tile = pl.BlockSpec((1, tm, d), lambda bi, ti: (bi, ti, 0))
    mod_spec = pl.BlockSpec((None, None, N_MOD, d), lambda bi, ti: (layer, bi, 0, 0))
    operands, specs = zip(*params)
    return pl.pallas_call(
        body,
        grid=(bsz, t // tm),
        in_specs=[tile, mod_spec, *specs],
        out_specs=tile,
        out_shape=jax.ShapeDtypeStruct(x.shape, F32),
        scratch_shapes=scratch_shapes,
        compiler_params=pltpu.CompilerParams(
            dimension_semantics=("arbitrary", "arbitrary"), vmem_limit_bytes=VMEM_LIMIT_BYTES),
        name=name,
    )(x, mod, *operands)


def _mod_kernel(ct_ref, w_ref, b_ref, o_ref):
    cond = _silu(ct_ref[...])
    d = o_ref.shape[3]
    rows = w_ref.shape[2] // d
    for step in range(o_ref.shape[2] // rows):
        @pl.when(pl.program_id(1) == step)
        def _():
            for r in range(rows):
                w = w_ref[0, :, r * d:(r + 1) * d]
                for bi in range(cond.shape[1]):
                    col = cond[:, bi:bi + 1]
                    out_row = step * rows + r
                    o_ref[0, bi, out_row:out_row + 1, :] = (
                        jnp.sum(w * col, axis=0, keepdims=True) + b_ref[0, :, r * d:(r + 1) * d])


def _modulation(c, ada_w, ada_b):
    depth, d, n = ada_w.shape
    bsz = c.shape[0]
    assert n == N_MOD * d and n % MOD_COL_BLOCK == 0 and MOD_COL_BLOCK % d == 0, ada_w.shape
    return pl.pallas_call(
        _mod_kernel,
        grid=(depth, n // MOD_COL_BLOCK),
        in_specs=[
            pl.BlockSpec((d, bsz), lambda i, j: (0, 0)),
            pl.BlockSpec((1, d, MOD_COL_BLOCK), lambda i, j: (i, 0, j)),
            pl.BlockSpec((1, 1, MOD_COL_BLOCK), lambda i, j: (i, 0, j)),
        ],
        out_specs=pl.BlockSpec((1, bsz, n // d, d), lambda i, j: (i, 0, 0, 0)),
        out_shape=jax.ShapeDtypeStruct((depth, bsz, n // d, d), F32),
        compiler_params=pltpu.CompilerParams(
            dimension_semantics=("arbitrary", "arbitrary"), vmem_limit_bytes=VMEM_LIMIT_BYTES),
        name="adaln_modulation",
    )(c.T, ada_w, ada_b.reshape(depth, 1, n))


def _hgrn_kernel(x_ref, mod_ref, preg_ref, postg_ref, win_hbm, lbl_ref, gn_ref, wout_hbm, o_ref,
                 win_ref, win_stage, win_sems, wout_ref, wout_stage, wout_sems,
                 state_ref, next_state_ref, q_s, k_s, b_s, v32_s, gate_s, vb_s, oh_s, *, layer):
    tm, d = x_ref.shape[1], x_ref.shape[2]
    c_len = HGRN_CHUNK
    n_chunks = tm // c_len
    j = layer // 2

    @pl.when(_first_step())
    def _():
        _fetch_as_bf16(win_hbm.at[j], win_ref, win_stage, win_sems)
        _fetch_as_bf16(wout_hbm.at[j], wout_ref, wout_stage, wout_sems)

    @pl.when(pl.program_id(1) == 0)
    def _():
        state_ref[...] = jnp.zeros_like(state_ref)

    x = x_ref[0]
    mod = mod_ref[...]
    shift, scale = mod[0:1], mod[1:2]
    h = _rms(x, preg_ref[layer:layer + 1, :] * (1.0 + scale)) + shift
    proj = _dot(h.astype(BF16), win_ref[...])

    logits = lbl_ref[...]
    ex = jnp.exp(logits - jnp.max(logits, axis=0, keepdims=True))
    lb = jnp.sum(ex[:layer + 1], axis=0, keepdims=True) / jnp.sum(ex, axis=0, keepdims=True)

    q = _silu(proj[:, :d])
    f = lb + (1.0 - lb) * _sigmoid(proj[:, d:2 * d])
    k = 1.0 - f
    logf = jnp.log(f)
    v = proj[:, 2 * d:3 * d]
    gate_s[...] = _silu(proj[:, 3 * d:])

    blk = CUMSUM_BLOCK
    row = lax.broadcasted_iota(jnp.int32, (blk, blk), 0)
    col = lax.broadcasted_iota(jnp.int32, (blk, blk), 1)
    tri = jnp.where((col <= row) & (row // c_len == col // c_len), 1.0, 0.0).astype(BF16)
    hi = logf.astype(BF16)
    rest = logf - hi.astype(F32)
    mid = rest.astype(BF16)
    lo = (rest - mid.astype(F32)).astype(BF16)
    max_decay = None
    for r in range(tm // blk):
        rows = slice(r * blk, (r + 1) * blk)
        b = _dot(tri, hi[rows]) + _dot(tri, mid[rows]) + _dot(tri, lo[rows])
        b_s[rows, :] = b
        block_max = jnp.max(-b)
        max_decay = block_max if max_decay is None else jnp.maximum(max_decay, block_max)

    q_s[...] = q
    k_s[...] = k
    v32_s[...] = v
    vb_s[...] = v.astype(BF16)
    bounded = max_decay <= MAX_CHUNK_LOG_DECAY

    causal = (lax.broadcasted_iota(jnp.int32, (c_len, c_len), 1)
              <= lax.broadcasted_iota(jnp.int32, (c_len, c_len), 0))
    row_idx = lax.broadcasted_iota(jnp.int32, (c_len, HEAD_DIM), 0)
    sub_idx = lax.broadcasted_iota(jnp.int32, (SUBLANES, HEAD_DIM), 0)

    def _factorised():
        for hh in range(HEADS):
            cs = slice(hh * HEAD_DIM, (hh + 1) * HEAD_DIM)
            st = state_ref[hh]
            for c in range(n_chunks):
                rows = slice(c * c_len, (c + 1) * c_len)
                qh, kh, bh, vb = q_s[rows, cs], k_s[rows, cs], b_s[rows, cs], vb_s[rows, cs]
                b_end = bh[c_len - 1:c_len, :]
                anchor = 0.5 * b_end
                e_anchor = jnp.exp(anchor)
                qt = (qh * jnp.exp(bh - anchor)).astype(BF16)
                kt = (kh * jnp.exp(anchor - bh)).astype(BF16)
                scores = jnp.where(causal, _dot_nt(qt, kt), 0.0).astype(BF16)
                oh_s[rows, cs] = _dot(scores, vb) + _dot(qt, (st * e_anchor).T.astype(BF16))
                st = st * jnp.exp(b_end) + _dot_tn(vb, kt) * e_anchor
            next_state_ref[hh] = st

    def _pairwise():
        def chunk_body(c, carry):
            r0 = pl.multiple_of(c * c_len, c_len)
            rows = pl.ds(r0, c_len)
            for hh in range(HEADS):
                cs = slice(hh * HEAD_DIM, (hh + 1) * HEAD_DIM)
                qh, kh, bh = q_s[rows, cs], k_s[rows, cs], b_s[rows, cs]
                qt, vb = (qh * jnp.exp(bh)).astype(BF16), vb_s[rows, cs]
                b_end = bh[c_len - 1:c_len, :]
                st = state_ref[hh]

                def s_body(s, o_acc):
                    grp = pl.ds(pl.multiple_of(r0 + (s // SUBLANES) * SUBLANES, SUBLANES), SUBLANES)
                    pick = sub_idx == s % SUBLANES
                    k_row = jnp.sum(jnp.where(pick, k_s[grp, cs], 0.0), axis=0, keepdims=True)
                    b_row = jnp.sum(jnp.where(pick, b_s[grp, cs], 0.0), axis=0, keepdims=True)
                    v_row = jnp.sum(jnp.where(pick, v32_s[grp, cs], 0.0), axis=0, keepdims=True)
                    decay = jnp.exp(jnp.minimum(bh - b_row, 0.0))
                    p = jnp.where(row_idx >= s, qh * decay * k_row, 0.0)
                    return o_acc + jnp.sum(p, axis=-1, keepdims=True) * v_row

                oh_s[rows, cs] = lax.fori_loop(0, c_len, s_body, _dot_nt(qt, st.astype(BF16)))
                k_hat = (kh * jnp.exp(b_end - bh)).astype(BF16)
                state_ref[hh] = st * jnp.exp(b_end) + _dot_tn(vb, k_hat)
            return carry

        lax.fori_loop(0, n_chunks, chunk_body, 0)

    def _finish():
        gn = gn_ref[j:j + 1, :]
        heads_out = []
        for hh in range(HEADS):
            cs = slice(hh * HEAD_DIM, (hh + 1) * HEAD_DIM)
            heads_out.append(_rms(oh_s[:, cs], gn))
        o = jnp.concatenate(heads_out, axis=-1) * gate_s[...]
        y = _dot(o.astype(BF16), wout_ref[...])
        o_ref[0] = x_ref[0] + _rms(y, postg_ref[layer:layer + 1, :] * mod_ref[2:3, :])

    _factorised()
    _finish()

    @pl.when(bounded)
    def _commit():
        state_ref[...] = next_state_ref[...]

    @pl.when(jnp.logical_not(bounded))
    def _redo():
        _pairwise()
        _finish()


def _hgrn_sublayer(x, mod, layer, pre_g, post_g, w_in, lb_logits, gnorm_g, w_out):
    d = x.shape[2]
    tm = HGRN_TOKENS_PER_STEP
    assert tm % CUMSUM_BLOCK == 0 and CUMSUM_BLOCK % HGRN_CHUNK == 0
    assert w_in.shape[1:] == (d, 4 * d) and d == HEADS * HEAD_DIM, w_in.shape
    params = [_whole(pre_g), _whole(post_g), _in_hbm(w_in), _whole(lb_logits), _whole(gnorm_g), _in_hbm(w_out)]
    scratch = [
        *_weight_scratch(*w_in.shape[1:]),
        *_weight_scratch(*w_out.shape[1:]),
        pltpu.VMEM((HEADS, HEAD_DIM, HEAD_DIM), F32),
        pltpu.VMEM((HEADS, HEAD_DIM, HEAD_DIM), F32),
        pltpu.VMEM((tm, d), F32),
        pltpu.VMEM((tm, d), F32),
        pltpu.VMEM((tm, d), F32),
        pltpu.VMEM((tm, d), F32),
        pltpu.VMEM((tm, d), F32),
        pltpu.VMEM((tm, d), BF16),
        pltpu.VMEM((tm, d), F32),
    ]
    return _sublayer_call(functools.partial(_hgrn_kernel, layer=layer), f"hgrn_sublayer_{layer}",
                          x, mod, layer, params, scratch, tm)


def _conformer_kernel(x_ref, mod_ref, preg_ref, postg_ref, win_hbm, bin_ref, dww_ref, dwb_ref,
                      lng_ref, lnb_ref, wout_hbm, bout_ref, o_ref,
                      win_ref, win_stage, win_sems, wout_ref, wout_stage, wout_sems, ext_ref, conv_s, *, layer):
    tm, d = x_ref.shape[1], x_ref.shape[2]
    halo = CONV_HALO
    j = layer // 2
    dww_ref = dww_ref.at[j]

    def row(ref, i):
        return ref[i:i + 1, :]

    @pl.when(_first_step())
    def _():
        _fetch_as_bf16(win_hbm.at[j], win_ref, win_stage, win_sems, scale=0.5)
        _fetch_as_bf16(wout_hbm.at[j], wout_ref, wout_stage, wout_sems)

    @pl.when(pl.program_id(1) == 0)
    def _():
        ext_ref[0:halo, :] = jnp.zeros((halo, d), F32)

    x = x_ref[0]
    mod = mod_ref[...]
    h = _rms(x, row(preg_ref, layer) * (1.0 + mod[1:2])) + mod[0:1]
    half_u = _dot(h.astype(BF16), win_ref[...]) + 0.5 * row(bin_ref, j)
    ext_ref[halo:halo + tm, :] = half_u[:, :d] * jnp.tanh(half_u[:, d:]) + half_u[:, :d]

    rb, cw = CONV_ROW_BLOCK, CONV_COL_BLOCK
    first = halo - (CONV_WIDTH - 1)

    def row_block(i, carry):
        r0 = pl.multiple_of(i * rb, rb)
        for cb in range(d // cw):
            cols = slice(cb * cw, (cb + 1) * cw)
            y = jnp.broadcast_to(dwb_ref[j:j + 1, cols], (rb, cw))
            for rr in range(SUBLANES):
                n = rb if rr == 0 else rb + SUBLANES
                z = None
                for m in range(halo // SUBLANES + 1):
                    kk = SUBLANES * m + rr - first
                    if 0 <= kk < CONV_WIDTH:
                        start = pl.multiple_of(r0 + SUBLANES * m, SUBLANES)
                        term = dww_ref[kk:kk + 1, cols] * ext_ref[pl.ds(start, n), cols]
                        z = term if z is None else z + term
                y = y + (z if rr == 0 else z[rr:rr + rb, :])
            conv_s[pl.ds(r0, rb), cols] = y
        return carry

    lax.fori_loop(0, tm // rb, row_block, 0)
    ext_ref[0:halo, :] = ext_ref[tm:tm + halo, :]

    c = conv_s[...]
    mu = jnp.mean(c, axis=-1, keepdims=True)
    cc = c - mu
    var = jnp.mean(cc * cc, axis=-1, keepdims=True)
    half = cc * lax.rsqrt(var + EPS) * (0.5 * row(lng_ref, j)) + 0.5 * row(lnb_ref, j)
    y = _dot((half * jnp.tanh(half) + half).astype(BF16), wout_ref[...]) + row(bout_ref, j)
    o_ref[0] = x + _rms(y, row(postg_ref, layer) * mod[2:3])


def _conformer_sublayer(x, mod, layer, pre_g, post_g, w_in, b_in, dw_w, dw_b, ln_g, ln_b, w_out, b_out):
    d = x.shape[2]
    tm = CONFORMER_TOKENS_PER_STEP
    assert tm % CONV_ROW_BLOCK == 0 and d % CONV_COL_BLOCK == 0 and CONV_WIDTH - 1 <= CONV_HALO <= tm
    assert dw_w.shape[1:] == (CONV_WIDTH, d), dw_w.shape
    params = [_whole(pre_g), _whole(post_g), _in_hbm(w_in), _whole(b_in), _whole(dw_w), _whole(dw_b),
              _whole(ln_g), _whole(ln_b), _in_hbm(w_out), _whole(b_out)]
    scratch = [
        *_weight_scratch(*w_in.shape[1:]),
        *_weight_scratch(*w_out.shape[1:]),
        pltpu.VMEM((CONV_HALO + tm, d), F32),
        pltpu.VMEM((tm, d), F32),
    ]
    return _sublayer_call(functools.partial(_conformer_kernel, layer=layer), "conformer_sublayer",
                          x, mod, layer, params, scratch, tm)


def _ffn_kernel(x_ref, mod_ref, preg_ref, postg_ref, wup_hbm, dww_ref, dwb_ref, wdown_hbm, o_ref,
                wup_ref, wup_stage, wup_sems, wdown_ref, wdown_stage, wdown_sems, hist_ref, act_s, *, layer):
    tm = x_ref.shape[1]
    fc = FFN_COL_BLOCK
    dww_ref = dww_ref.at[layer]

    @pl.when(_first_step())
    def _():
        _fetch_as_bf16(wup_hbm.at[layer], wup_ref, wup_stage, wup_sems)
        _fetch_as_bf16(wdown_hbm.at[layer], wdown_ref, wdown_stage, wdown_sems)

    @pl.when(pl.program_id(1) == 0)
    def _():
        hist_ref[...] = jnp.zeros_like(hist_ref)

    x = x_ref[0]
    mod = mod_ref[...]
    h = (_rms(x, preg_ref[layer:layer + 1, :] * (1.0 + mod[4:5])) + mod[3:4]).astype(BF16)

    def conv_block(c0):
        cols = slice(c0, c0 + fc)
        u = _dot(h, wup_ref[:, cols])
        ext = jnp.concatenate([hist_ref[:, cols], u], axis=0)
        hist_ref[:, cols] = u[tm - SUBLANES:, :]
        last = FFN_CONV_WIDTH - 1
        out = dww_ref[last:last + 1, cols] * u + dwb_ref[layer:layer + 1, cols]
        for lag in range(1, FFN_CONV_WIDTH):
            shifted = pltpu.roll(ext, shift=lag, axis=0)[SUBLANES:, :]
            out = out + dww_ref[last - lag:last - lag + 1, cols] * shifted
        return out

    for j in range(D_FF // fc):
        a = conv_block(j * fc)
        g = conv_block(D_FF + j * fc)
        act_s[:, j * fc:(j + 1) * fc] = (_silu(a) * g).astype(BF16)
    y = _dot(act_s[...], wdown_ref[...])
    o_ref[0] = x + _rms(y, postg_ref[layer:layer + 1, :] * mod[5:6])


def _ffn_sublayer(x, mod, layer, pre_g, post_g, w_up, dw_w, dw_b, w_down):
    tm = FFN_TOKENS_PER_STEP
    assert dw_w.shape[1:] == (FFN_CONV_WIDTH, 2 * D_FF) and FFN_CONV_WIDTH - 1 <= SUBLANES, dw_w.shape
    assert D_FF % FFN_COL_BLOCK == 0
    params = [_whole(pre_g), _whole(post_g), _in_hbm(w_up), _whole(dw_w), _whole(dw_b), _in_hbm(w_down)]
    scratch = [
        *_weight_scratch(*w_up.shape[1:]),
        *_weight_scratch(*w_down.shape[1:]),
        pltpu.VMEM((SUBLANES, 2 * D_FF), F32),
        pltpu.VMEM((tm, D_FF), BF16),
    ]
    return _sublayer_call(functools.partial(_ffn_kernel, layer=layer), "convffn_sublayer",
                          x, mod, layer, params, scratch, tm)


def kernel(x, c, ada_w, ada_b, pre_mix_g, post_mix_g, pre_ffn_g, post_ffn_g, hgrn_w_in, hgrn_lb_logits, hgrn_gnorm_g, hgrn_w_out, conv_w_in, conv_b_in, conv_dw_w, conv_dw_b, conv_ln_g, conv_ln_b, conv_w_out, conv_b_out, ffn_w_up, ffn_dw_w, ffn_dw_b, ffn_w_down):
    depth = ada_w.shape[0]
    mod = _modulation(c, ada_w, ada_b)
    for i in range(depth):
        if i % 2 == 0:
            x = _hgrn_sublayer(x, mod, i, pre_mix_g, post_mix_g, hgrn_w_in, hgrn_lb_logits, hgrn_gnorm_g,
                               hgrn_w_out)
        else:
            x = _conformer_sublayer(x, mod, i, pre_mix_g, post_mix_g, conv_w_in, conv_b_in, conv_dw_w,
                                    conv_dw_b, conv_ln_g, conv_ln_b, conv_w_out, conv_b_out)
        x = _ffn_sublayer(x, mod, i, pre_ffn_g, post_ffn_g, ffn_w_up, ffn_dw_w, ffn_dw_b, ffn_w_down)
    return x
```

```python
import functools

import jax
import jax.numpy as jnp
from jax import lax
from jax.experimental import pallas as pl
from jax.experimental.pallas import tpu as pltpu

F32 = jnp.float32
BF16 = jnp.bfloat16

D_MODEL = 1024
N_MOD = 6
HEADS = 8
HEAD_DIM = 128
D_FF = 2816
CONV_WIDTH = 31
FFN_CONV_WIDTH = 3
EPS = 1e-6

SUBLANES = 8
HGRN_TOKENS_PER_STEP = 512
CONFORMER_TOKENS_PER_STEP = 1024
FFN_TOKENS_PER_STEP = 1024
HGRN_CHUNK = 64
CUMSUM_BLOCK = 256
MAX_CHUNK_LOG_DECAY = 160.0
CONV_HALO = 32
CONV_ROW_BLOCK = 256
CONV_COL_BLOCK = 128
FFN_COL_BLOCK = 256
MOD_COL_BLOCK = 2048
WEIGHT_STAGE_BYTES = 2 * 1024 * 1024
VMEM_LIMIT_BYTES = 56 * 1024 * 1024


def _sigmoid(x):
    return 0.5 * jnp.tanh(0.5 * x) + 0.5


def _silu(x):
    half = 0.5 * x
    return half * jnp.tanh(half) + half


def _rms(x, g):
    return x * lax.rsqrt(jnp.mean(x * x, axis=-1, keepdims=True) + EPS) * g


def _dot(a, b):
    return jnp.dot(a, b, preferred_element_type=F32)


def _dot_nt(a, b):
    return lax.dot_general(a, b, (((1,), (1,)), ((), ())), preferred_element_type=F32)


def _dot_tn(a, b):
    return lax.dot_general(a, b, (((0,), (0,)), ((), ())), preferred_element_type=F32)


def _whole(arr):
    ndim = arr.ndim
    return arr, pl.BlockSpec(arr.shape, lambda bi, ti: (0,) * ndim, pipeline_mode=pl.Buffered(1))


def _in_hbm(arr):
    return arr, pl.BlockSpec(memory_space=pl.ANY)


def _stage_rows(rows, cols):
    limit = max(SUBLANES, WEIGHT_STAGE_BYTES // (4 * cols))
    return max(r for r in range(SUBLANES, rows + 1, SUBLANES) if rows % r == 0 and r <= limit)


def _weight_scratch(rows, cols):
    stage = _stage_rows(rows, cols)
    return [pltpu.VMEM((rows, cols), BF16), pltpu.VMEM((2, stage, cols), F32), pltpu.SemaphoreType.DMA((2,))]


def _fetch_as_bf16(src_hbm, dst, staging, sems, scale=None):
    stage = staging.shape[1]
    n_chunks = src_hbm.shape[0] // stage

    def chunk_copy(i):
        return pltpu.make_async_copy(src_hbm.at[pl.ds(i * stage, stage), :], staging.at[i % 2], sems.at[i % 2])

    chunk_copy(0).start()
    for i in range(n_chunks):
        if i + 1 < n_chunks:
            chunk_copy(i + 1).start()
        chunk_copy(i).wait()
        chunk = staging[i % 2]
        dst[i * stage:(i + 1) * stage, :] = (chunk if scale is None else chunk * scale).astype(BF16)


def _first_step():
    return (pl.program_id(0) == 0) & (pl.program_id(1) == 0)


def _sublayer_call(body, name, x, mod, layer, params, scratch_shapes, tm):
    bsz, t, d = x.shape
    assert t % tm == 0 and d == D_MODEL, (x.shape, tm)
    tile = pl.BlockSpec((1, tm, d), lambda bi, ti: (bi, ti, 0))
    mod_spec = pl.BlockSpec((None, None, N_MOD, d), lambda bi, ti: (layer, bi, 0, 0))
    operands, specs = zip(*params)
    return pl.pallas_call(
        body,
        grid=(bsz, t // tm),
        in_specs=[tile, mod_spec, *specs],
        out_specs=tile,
        out_shape=jax.ShapeDtypeStruct(x.shape, F32),
        scratch_shapes=scratch_shapes,
        compiler_params=pltpu.CompilerParams(
            dimension_semantics=("arbitrary", "arbitrary"), vmem_limit_bytes=VMEM_LIMIT_BYTES),
        name=name,
    )(x, mod, *operands)


def _mod_kernel(ct_ref, w_ref, b_ref, o_ref):
    cond = _silu(ct_ref[...])
    d = o_ref.shape[3]
    rows = w_ref.shape[2] // d
    for step in range(o_ref.shape[2] // rows):
        @pl.when(pl.program_id(1) == step)
        def _():
            for r in range(rows):
                w = w_ref[0, :, r * d:(r + 1) * d]
                for bi in range(cond.shape[1]):
                    col = cond[:, bi:bi + 1]
                    out_row = step * rows + r
                    o_ref[0, bi, out_row:out_row + 1, :] = (
                        jnp.sum(w * col, axis=0, keepdims=True) + b_ref[0, :, r * d:(r + 1) * d])


def _modulation(c, ada_w, ada_b):
    depth, d, n = ada_w.shape
    bsz = c.shape[0]
    assert n == N_MOD * d and n % MOD_COL_BLOCK == 0 and MOD_COL_BLOCK % d == 0, ada_w.shape
    return pl.pallas_call(
        _mod_kernel,
        grid=(depth, n // MOD_COL_BLOCK),
        in_specs=[
            pl.BlockSpec((d, bsz), lambda i, j: (0, 0)),
            pl.BlockSpec((1, d, MOD_COL_BLOCK), lambda i, j: (i, 0, j)),
            pl.BlockSpec((1, 1, MOD_COL_BLOCK), lambda i, j: (i, 0, j)),
        ],
        out_specs=pl.BlockSpec((1, bsz, n // d, d), lambda i, j: (i, 0, 0, 0)),
        out_shape=jax.ShapeDtypeStruct((depth, bsz, n // d, d), F32),
        compiler_params=pltpu.CompilerParams(
            dimension_semantics=("arbitrary", "arbitrary"), vmem_limit_bytes=VMEM_LIMIT_BYTES),
        name="adaln_modulation",
    )(c.T, ada_w, ada_b.reshape(depth, 1, n))


def _hgrn_kernel(x_ref, mod_ref, preg_ref, postg_ref, win_hbm, lbl_ref, gn_ref, wout_hbm, o_ref,
                 win_ref, win_stage, win_sems, wout_ref, wout_stage, wout_sems,
                 state_ref, next_state_ref, q_s, k_s, b_s, v32_s, gate_s, vb_s, oh_s, *, layer):
    tm, d = x_ref.shape[1], x_ref.shape[2]
    c_len = HGRN_CHUNK
    n_chunks = tm // c_len
    j = layer // 2

    @pl.when(_first_step())
    def _():
        _fetch_as_bf16(win_hbm.at[j], win_ref, win_stage, win_sems)
        _fetch_as_bf16(wout_hbm.at[j], wout_ref, wout_stage, wout_sems)

    @pl.when(pl.program_id(1) == 0)
    def _():
        state_ref[...] = jnp.zeros_like(state_ref)

    x = x_ref[0]
    mod = mod_ref[...]
    shift, scale = mod[0:1], mod[1:2]
    h = _rms(x, preg_ref[layer:layer + 1, :] * (1.0 + scale)) + shift
    proj = _dot(h.astype(BF16), win_ref[...])

    logits = lbl_ref[...]
    ex = jnp.exp(logits - jnp.max(logits, axis=0, keepdims=True))
    lb = jnp.sum(ex[:layer + 1], axis=0, keepdims=True) / jnp.sum(ex, axis=0, keepdims=True)

    q = _silu(proj[:, :d])
    f = lb + (1.0 - lb) * _sigmoid(proj[:, d:2 * d])
    k = 1.0 - f
    logf = jnp.log(f)
    v = proj[:, 2 * d:3 * d]
    gate_s[...] = _silu(proj[:, 3 * d:])

    blk = CUMSUM_BLOCK
    row = lax.broadcasted_iota(jnp.int32, (blk, blk), 0)
    col = lax.broadcasted_iota(jnp.int32, (blk, blk), 1)
    tri = jnp.where((col <= row) & (row // c_len == col // c_len), 1.0, 0.0).astype(BF16)
    hi = logf.astype(BF16)
    rest = logf - hi.astype(F32)
    mid = rest.astype(BF16)
    lo = (rest - mid.astype(F32)).astype(BF16)
    max_decay = None
    for r in range(tm // blk):
        rows = slice(r * blk, (r + 1) * blk)
        b = _dot(tri, hi[rows]) + _dot(tri, mid[rows]) + _dot(tri, lo[rows])
        b_s[rows, :] = b
        block_max = jnp.max(-b)
        max_decay = block_max if max_decay is None else jnp.maximum(max_decay, block_max)

    q_s[...] = q
    k_s[...] = k
    v32_s[...] = v
    vb_s[...] = v.astype(BF16)
    bounded = max_decay <= MAX_CHUNK_LOG_DECAY

    causal = (lax.broadcasted_iota(jnp.int32, (c_len, c_len), 1)
              <= lax.broadcasted_iota(jnp.int32, (c_len, c_len), 0))
    row_idx = lax.broadcasted_iota(jnp.int32, (c_len, HEAD_DIM), 0)
    sub_idx = lax.broadcasted_iota(jnp.int32, (SUBLANES, HEAD_DIM), 0)

    def _factorised():
        for hh in range(HEADS):
            cs = slice(hh * HEAD_DIM, (hh + 1) * HEAD_DIM)
            st = state_ref[hh]
            for c in range(n_chunks):
                rows = slice(c * c_len, (c + 1) * c_len)
                qh, kh, bh, vb = q_s[rows, cs], k_s[rows, cs], b_s[rows, cs], vb_s[rows, cs]
                b_end = bh[c_len - 1:c_len, :]
                anchor = 0.5 * b_end
                e_anchor = jnp.exp(anchor)
                qt = (qh * jnp.exp(bh - anchor)).astype(BF16)
                kt = (kh * jnp.exp(anchor - bh)).astype(BF16)
                scores = jnp.where(causal, _dot_nt(qt, kt), 0.0).astype(BF16)
                lhs = jnp.concatenate([qt, scores], axis=1)
                rhs = jnp.concatenate([(st * e_anchor).T.astype(BF16), vb], axis=0)
                oh_s[rows, cs] = _dot(lhs, rhs)
                st = st * jnp.exp(b_end) + _dot_tn(vb, kt) * e_anchor
            next_state_ref[hh] = st

    def _pairwise():
        def chunk_body(c, carry):
            r0 = pl.multiple_of(c * c_len, c_len)
            rows = pl.ds(r0, c_len)
            for hh in range(HEADS):
                cs = slice(hh * HEAD_DIM, (hh + 1) * HEAD_DIM)
                qh, kh, bh = q_s[rows, cs], k_s[rows, cs], b_s[rows, cs]
                qt, vb = (qh * jnp.exp(bh)).astype(BF16), vb_s[rows, cs]
                b_end = bh[c_len - 1:c_len, :]
                st = state_ref[hh]

                def s_body(s, o_acc):
                    grp = pl.ds(pl.multiple_of(r0 + (s // SUBLANES) * SUBLANES, SUBLANES), SUBLANES)
                    pick = sub_idx == s % SUBLANES
                    k_row = jnp.sum(jnp.where(pick, k_s[grp, cs], 0.0), axis=0, keepdims=True)
                    b_row = jnp.sum(jnp.where(pick, b_s[grp, cs], 0.0), axis=0, keepdims=True)
                    v_row = jnp.sum(jnp.where(pick, v32_s[grp, cs], 0.0), axis=0, keepdims=True)
                    decay = jnp.exp(jnp.minimum(bh - b_row, 0.0))
                    p = jnp.where(row_idx >= s, qh * decay * k_row, 0.0)
                    return o_acc + jnp.sum(p, axis=-1, keepdims=True) * v_row

                oh_s[rows, cs] = lax.fori_loop(0, c_len, s_body, _dot_nt(qt, st.astype(BF16)))
                k_hat = (kh * jnp.exp(b_end - bh)).astype(BF16)
                state_ref[hh] = st * jnp.exp(b_end) + _dot_tn(vb, k_hat)
            return carry

        lax.fori_loop(0, n_chunks, chunk_body, 0)

    def _finish():
        gn = gn_ref[j:j + 1, :]
        heads_out = []
        for hh in range(HEADS):
            cs = slice(hh * HEAD_DIM, (hh + 1) * HEAD_DIM)
            heads_out.append(_rms(oh_s[:, cs], gn))
        o = jnp.concatenate(heads_out, axis=-1) * gate_s[...]
        y = _dot(o.astype(BF16), wout_ref[...])
        o_ref[0] = x_ref[0] + _rms(y, postg_ref[layer:layer + 1, :] * mod_ref[2:3, :])

    _factorised()
    _finish()

    @pl.when(bounded)
    def _commit():
        state_ref[...] = next_state_ref[...]

    @pl.when(jnp.logical_not(bounded))
    def _redo():
        _pairwise()
        _finish()


def _hgrn_sublayer(x, mod, layer, pre_g, post_g, w_in, lb_logits, gnorm_g, w_out):
    d = x.shape[2]
    tm = HGRN_TOKENS_PER_STEP
    assert tm % CUMSUM_BLOCK == 0 and CUMSUM_BLOCK % HGRN_CHUNK == 0
    assert w_in.shape[1:] == (d, 4 * d) and d == HEADS * HEAD_DIM, w_in.shape
    params = [_whole(pre_g), _whole(post_g), _in_hbm(w_in), _whole(lb_logits), _whole(gnorm_g), _in_hbm(w_out)]
    scratch = [
        *_weight_scratch(*w_in.shape[1:]),
        *_weight_scratch(*w_out.shape[1:]),
        pltpu.VMEM((HEADS, HEAD_DIM, HEAD_DIM), F32),
        pltpu.VMEM((HEADS, HEAD_DIM, HEAD_DIM), F32),
        pltpu.VMEM((tm, d), F32),
        pltpu.VMEM((tm, d), F32),
        pltpu.VMEM((tm, d), F32),
        pltpu.VMEM((tm, d), F32),
        pltpu.VMEM((tm, d), F32),
        pltpu.VMEM((tm, d), BF16),
        pltpu.VMEM((tm, d), F32),
    ]
    return _sublayer_call(functools.partial(_hgrn_kernel, layer=layer), f"hgrn_sublayer_{layer}",
                          x, mod, layer, params, scratch, tm)


def _conformer_kernel(x_ref, mod_ref, preg_ref, postg_ref, win_hbm, bin_ref, dww_ref, dwb_ref,
                      lng_ref, lnb_ref, wout_hbm, bout_ref, o_ref,
                      win_ref, win_stage, win_sems, wout_ref, wout_stage, wout_sems, ext_ref, conv_s, *, layer):
    tm, d = x_ref.shape[1], x_ref.shape[2]
    halo = CONV_HALO
    j = layer // 2
    dww_ref = dww_ref.at[j]

    def row(ref, i):
        return ref[i:i + 1, :]

    @pl.when(_first_step())
    def _():
        _fetch_as_bf16(win_hbm.at[j], win_ref, win_stage, win_sems, scale=0.5)
        _fetch_as_bf16(wout_hbm.at[j], wout_ref, wout_stage, wout_sems)

    @pl.when(pl.program_id(1) == 0)
    def _():
        ext_ref[0:halo, :] = jnp.zeros((halo, d), F32)

    x = x_ref[0]
    mod = mod_ref[...]
    h = _rms(x, row(preg_ref, layer) * (1.0 + mod[1:2])) + mod[0:1]
    half_u = _dot(h.astype(BF16), win_ref[...]) + 0.5 * row(bin_ref, j)
    ext_ref[halo:halo + tm, :] = half_u[:, :d] * jnp.tanh(half_u[:, d:]) + half_u[:, :d]

    rb, cw = CONV_ROW_BLOCK, CONV_COL_BLOCK
    first = halo - (CONV_WIDTH - 1)

    def row_block(i, carry):
        r0 = pl.multiple_of(i * rb, rb)
        for cb in range(d // cw):
            cols = slice(cb * cw, (cb + 1) * cw)
            y = jnp.broadcast_to(dwb_ref[j:j + 1, cols], (rb, cw))
            for rr in range(SUBLANES):
                n = rb if rr == 0 else rb + SUBLANES
                z = None
                for m in range(halo // SUBLANES + 1):
                    kk = SUBLANES * m + rr - first
                    if 0 <= kk < CONV_WIDTH:
                        start = pl.multiple_of(r0 + SUBLANES * m, SUBLANES)
                        term = dww_ref[kk:kk + 1, cols] * ext_ref[pl.ds(start, n), cols]
                        z = term if z is None else z + term
                y = y + (z if rr == 0 else z[rr:rr + rb, :])
            conv_s[pl.ds(r0, rb), cols] = y
        return carry

    lax.fori_loop(0, tm // rb, row_block, 0)
    ext_ref[0:halo, :] = ext_ref[tm:tm + halo, :]

    c = conv_s[...]
    mu = jnp.mean(c, axis=-1, keepdims=True)
    cc = c - mu
    var = jnp.mean(cc * cc, axis=-1, keepdims=True)
    half = cc * lax.rsqrt(var + EPS) * (0.5 * row(lng_ref, j)) + 0.5 * row(lnb_ref, j)
    y = _dot((half * jnp.tanh(half) + half).astype(BF16), wout_ref[...]) + row(bout_ref, j)
    o_ref[0] = x + _rms(y, row(postg_ref, layer) * mod[2:3])


def _conformer_sublayer(x, mod, layer, pre_g, post_g, w_in, b_in, dw_w, dw_b, ln_g, ln_b, w_out, b_out):
    d = x.shape[2]
    tm = CONFORMER_TOKENS_PER_STEP
    assert tm % CONV_ROW_BLOCK == 0 and d % CONV_COL_BLOCK == 0 and CONV_WIDTH - 1 <= CONV_HALO <= tm
    assert dw_w.shape[1:] == (CONV_WIDTH, d), dw_w.shape
    params = [_whole(pre_g), _whole(post_g), _in_hbm(w_in), _whole(b_in), _whole(dw_w), _whole(dw_b),
              _whole(ln_g), _whole(ln_b), _in_hbm(w_out), _whole(b_out)]
    scratch = [
        *_weight_scratch(*w_in.shape[1:]),
        *_weight_scratch(*w_out.shape[1:]),
        pltpu.VMEM((CONV_HALO + tm, d), F32),
        pltpu.VMEM((tm, d), F32),
    ]
    return _sublayer_call(functools.partial(_conformer_kernel, layer=layer), "conformer_sublayer",
                          x, mod, layer, params, scratch, tm)


def _ffn_kernel(x_ref, mod_ref, preg_ref, postg_ref, wup_hbm, dww_ref, dwb_ref, wdown_hbm, o_ref,
                wup_ref, wup_stage, wup_sems, wdown_ref, wdown_stage, wdown_sems, hist_ref, act_s, *, layer):
    tm = x_ref.shape[1]
    fc = FFN_COL_BLOCK
    dww_ref = dww_ref.at[layer]

    @pl.when(_first_step())
    def _():
        _fetch_as_bf16(wup_hbm.at[layer], wup_ref, wup_stage, wup_sems)
        _fetch_as_bf16(wdown_hbm.at[layer], wdown_ref, wdown_stage, wdown_sems)

    @pl.when(pl.program_id(1) == 0)
    def _():
        hist_ref[...] = jnp.zeros_like(hist_ref)

    x = x_ref[0]
    mod = mod_ref[...]
    h = (_rms(x, preg_ref[layer:layer + 1, :] * (1.0 + mod[4:5])) + mod[3:4]).astype(BF16)

    def conv_block(c0):
        cols = slice(c0, c0 + fc)
        u = _dot(h, wup_ref[:, cols])
        ext = jnp.concatenate([hist_ref[:, cols], u], axis=0)
        hist_ref[:, cols] = u[tm - SUBLANES:, :]
        last = FFN_CONV_WIDTH - 1
        out = dww_ref[last:last + 1, cols] * u + dwb_ref[layer:layer + 1, cols]
        for lag in range(1, FFN_CONV_WIDTH):
            shifted = pltpu.roll(ext, shift=lag, axis=0)[SUBLANES:, :]
            out = out + dww_ref[last - lag:last - lag + 1, cols] * shifted
        return out

    for j in range(D_FF // fc):
        a = conv_block(j * fc)
        g = conv_block(D_FF + j * fc)
        act_s[:, j * fc:(j + 1) * fc] = (_silu(a) * g).astype(BF16)
    y = _dot(act_s[...], wdown_ref[...])
    o_ref[0] = x + _rms(y, postg_ref[layer:layer + 1, :] * mod[5:6])


def _ffn_sublayer(x, mod, layer, pre_g, post_g, w_up, dw_w, dw_b, w_down):
    tm = FFN_TOKENS_PER_STEP
    assert dw_w.shape[1:] == (FFN_CONV_WIDTH, 2 * D_FF) and FFN_CONV_WIDTH - 1 <= SUBLANES, dw_w.shape
    assert D_FF % FFN_COL_BLOCK == 0
    params = [_whole(pre_g), _whole(post_g), _in_hbm(w_up), _whole(dw_w), _whole(dw_b), _in_hbm(w_down)]
    scratch = [
        *_weight_scratch(*w_up.shape[1:]),
        *_weight_scratch(*w_down.shape[1:]),
        pltpu.VMEM((SUBLANES, 2 * D_FF), F32),
        pltpu.VMEM((tm, D_FF), BF16),
    ]
    return _sublayer_call(functools.partial(_ffn_kernel, layer=layer), "convffn_sublayer",
                          x, mod, layer, params, scratch, tm)


def kernel(x, c, ada_w, ada_b, pre_mix_g, post_mix_g, pre_ffn_g, post_ffn_g, hgrn_w_in, hgrn_lb_logits, hgrn_gnorm_g, hgrn_w_out, conv_w_in, conv_b_in, conv_dw_w, conv_dw_b, conv_ln_g, conv_ln_b, conv_w_out, conv_b_out, ffn_w_up, ffn_dw_w, ffn_dw_b, ffn_w_down):
    depth = ada_w.shape[0]
    mod = _modulation(c, ada_w, ada_b)
    for i in range(depth):
        if i % 2 == 0:
            x = _hgrn_sublayer(x, mod, i, pre_mix_g, post_mix_g, hgrn_w_in, hgrn_lb_logits, hgrn_gnorm_g,
                               hgrn_w_out)
        else:
            x = _conformer_sublayer(x, mod, i, pre_mix_g, post_mix_g, conv_w_in, conv_b_in, conv_dw_w,
                                    conv_dw_b, conv_ln_g, conv_ln_b, conv_w_out, conv_b_out)
        x = _ffn_sublayer(x, mod, i, pre_ffn_g, post_ffn_g, ffn_w_up, ffn_dw_w, ffn_dw_b, ffn_w_down)
    return x
```
